```python
import math
import jax, jax.numpy as jnp
from jax import lax
import numpy as np

D_MODEL = 1024
BATCH = 32
SEQ = 256
DEPTH = 1
DEC_BATCH = 8
DEC_SEQ = 1024
PAST_LEN = 512

GRID_W = 64
D_RNN = 1024
N_RNN_HEADS = 4
RNN_HEAD_DIM = D_RNN // N_RNN_HEADS
RNN_CONV_W = 4
RNN_CONV_LEFT = 2
RGLRU_C = 8.0
D_HY = 1024
HY_ORDER = 2
HY_CONV_W = 3
HY_CONV_LEFT = 1
HY_EMB_BANDS = 16
HY_EMB_DIM = 1 + 2 * HY_EMB_BANDS
HY_FILTER_HIDDEN = 64
HY_DECAY_TARGET = 1e-2
HY_FAST_DECAY_PCT = 0.3
HY_SLOW_DECAY_PCT = 1.5
HY_MIN_DECAY = math.log(HY_DECAY_TARGET) / HY_SLOW_DECAY_PCT
HY_MAX_DECAY = math.log(HY_DECAY_TARGET) / HY_FAST_DECAY_PCT
D_IN = 2 * D_RNN + 3 * D_HY + 2 * D_MODEL
N_KEYS = 128
N_EXPERTS = N_KEYS * N_KEYS
PEER_HEADS = 8
PEER_KEY_DIM = 256
PEER_HALF = PEER_KEY_DIM // 2
PEER_TOPK = 16
PEER_TOKEN_BLOCK = 128
DN_ALPHA = (2.0 * DEPTH) ** 0.25
DN_BETA = (8.0 * DEPTH) ** -0.25
LN_EPS = 1e-5

kernel_name = "hybrid_rglru_hyena_peer_prefix_step"


def layer_norm(x, g, b):
    xf = x.astype(jnp.float32)
    mu = jnp.mean(xf, axis=-1, keepdims=True)
    var = jnp.mean(jnp.square(xf - mu), axis=-1, keepdims=True)
    return ((xf - mu) * lax.rsqrt(var + LN_EPS) * g.astype(jnp.float32) + b.astype(jnp.float32)).astype(x.dtype)


def depthwise_conv_centred(x, w, b, pad_left):
    k_w, ch = w.shape
    y = lax.conv_general_dilated(x, w[:, None, :].astype(x.dtype), window_strides=(1,),
                                 padding=[(pad_left, k_w - 1 - pad_left)],
                                 dimension_numbers=("NWC", "WIO", "NWC"), feature_group_count=ch)
    return y + b.astype(x.dtype)


def grid_pos_embed(n_tokens):
    rows = n_tokens // GRID_W
    t = jnp.arange(rows * GRID_W)
    r = (t // GRID_W).astype(jnp.float32)
    col = (t % GRID_W).astype(jnp.float32)
    quarter = D_MODEL // 4
    omega = 1.0 / (10000.0 ** (jnp.arange(quarter, dtype=jnp.float32) / quarter))
    er = r[:, None] * omega[None, :]
    ec = col[:, None] * omega[None, :]
    return jnp.concatenate([jnp.sin(er), jnp.cos(er), jnp.sin(ec), jnp.cos(ec)], axis=-1)


def linear_scan(a, u, h0, reverse):
    def combine(left, right):
        a_l, b_l = left
        a_r, b_r = right
        return a_l * a_r, a_r * b_l + b_r
    a_cum, b_cum = lax.associative_scan(combine, (a, u), axis=1, reverse=reverse)
    return a_cum * h0[:, None, :] + b_cum


def rglru_bidir(x, gate_w, gate_b, lam, h0):
    bsz, seq_len, _ = x.shape
    xh = x.reshape(bsz, seq_len, N_RNN_HEADS, RNN_HEAD_DIM)
    gates = jnp.einsum("blhi,dghij->dgblhj", xh, gate_w).reshape(2, 2, bsz, seq_len, D_RNN)
    gates = jax.nn.sigmoid((gates + gate_b[:, :, None, None, :]).astype(jnp.float32))
    r_gate, i_gate = gates[:, 0], gates[:, 1]
    log_a = -RGLRU_C * r_gate * jax.nn.softplus(-lam.astype(jnp.float32))[:, None, None, :]
    a = jnp.exp(log_a)
    u = jnp.sqrt(-jnp.expm1(2.0 * log_a)) * (i_gate * x.astype(jnp.float32)[None])
    h0f = h0.astype(jnp.float32)
    h_fwd = linear_scan(a[0], u[0], h0f[:, 0], reverse=False)
    h_bwd = linear_scan(a[1], u[1], h0f[:, 1], reverse=True)
    final_state = jnp.stack([h_fwd[:, -1], h_bwd[:, 0]], axis=1)
    return h_fwd + h_bwd, final_state


def hyena_filter_spectra(seq_len, w1, b1, w2, b2, w3, b3, sin_freq):
    f32 = jnp.float32
    t_idx = jnp.arange(seq_len, dtype=f32)
    t_norm = t_idx / max(seq_len - 1, 1)
    w = 2.0 * math.pi * t_idx / seq_len
    bands = jnp.linspace(1e-4, HY_EMB_BANDS - 1, HY_EMB_BANDS, dtype=f32)
    fw = w[:, None] * bands[None, :]
    z = jnp.concatenate([t_norm[:, None], jnp.cos(fw), -jnp.sin(fw)], axis=-1)
    freq = sin_freq.astype(f32)
    hid = jnp.sin(freq * (z @ w1.astype(f32) + b1.astype(f32)))
    hid = jnp.sin(freq * (hid @ w2.astype(f32) + b2.astype(f32)))
    filt = (hid @ w3.astype(f32) + b3.astype(f32)).reshape(seq_len, 2, HY_ORDER, D_HY)
    deltas = jnp.linspace(HY_MIN_DECAY, HY_MAX_DECAY, D_HY, dtype=f32)
    decay = jnp.exp(-t_norm[:, None] * jnp.abs(deltas)[None, :])
    filt = filt * decay[:, None, None, :]
    k = jnp.concatenate([filt[:, 0], jnp.zeros((1, HY_ORDER, D_HY), f32), jnp.flip(filt[1:, 1], axis=0)], axis=0)
    return jnp.fft.rfft(k, axis=0)


def fft_long_conv(u, kf, skip):
    seq_len = u.shape[1]
    y = jnp.fft.irfft(jnp.fft.rfft(u, n=2 * seq_len, axis=1) * kf[None], n=2 * seq_len, axis=1)[:, :seq_len]
    return y + u * skip


def hyena_branch(hy, p):
    hy = depthwise_conv_centred(hy, p["hy_conv_w"], p["hy_conv_b"], HY_CONV_LEFT).astype(jnp.float32)
    x1, x2, v = jnp.split(hy, 3, axis=-1)
    kf = hyena_filter_spectra(hy.shape[1], p["hy_ffn_w1"], p["hy_ffn_b1"], p["hy_ffn_w2"], p["hy_ffn_b2"],
                              p["hy_ffn_w3"], p["hy_ffn_b3"], p["hy_sin_freq"])
    skip = p["hy_skip"].astype(jnp.float32)
    z = x1 * fft_long_conv(v, kf[:, 0], skip[0])
    z = x2 * fft_long_conv(z, kf[:, 1], skip[1])
    return z


def token_mixer(h, h0, p):
    dt = h.dtype
    proj = h @ p["w_in"]
    rnn_x, rnn_g, hy, g_merge = jnp.split(proj, [D_RNN, 2 * D_RNN, 2 * D_RNN + 3 * D_HY], axis=-1)
    rnn_x = depthwise_conv_centred(rnn_x, p["rnn_conv_w"], p["rnn_conv_b"], RNN_CONV_LEFT)
    y_rnn, final_state = rglru_bidir(rnn_x, p["rnn_gate_w"], p["rnn_gate_b"], p["rnn_lambda"], h0)
    y_a = (y_rnn.astype(dt) * jax.nn.gelu(rnn_g)) @ p["w_branch_a"]
    y_b = hyena_branch(hy, p).astype(dt) @ p["w_branch_b"]
    g_a, g_b = jnp.split(jax.nn.sigmoid(g_merge), 2, axis=-1)
    return (g_a * y_a + g_b * y_b) @ p["w_out"], final_state


def peer(h, w_query, sub_keys, u_tab, v_tab):
    bsz, seq_len, dm = h.shape
    n_tok = bsz * seq_len
    xt = h.reshape(n_tok // PEER_TOKEN_BLOCK, PEER_TOKEN_BLOCK, dm)

    def block(xb):
        q = (xb @ w_query).reshape(PEER_TOKEN_BLOCK, PEER_HEADS, 2, PEER_HALF)
        s = jnp.einsum("thpk,pnk->thpn", q, sub_keys).astype(jnp.float32)
        top_s, top_i = lax.top_k(s, PEER_TOPK)
        cand_s = top_s[:, :, 0, :, None] + top_s[:, :, 1, None, :]
        cand_i = top_i[:, :, 0, :, None] * N_KEYS + top_i[:, :, 1, None, :]
        cand_s = cand_s.reshape(PEER_TOKEN_BLOCK, PEER_HEADS, PEER_TOPK * PEER_TOPK)
        cand_i = cand_i.reshape(PEER_TOKEN_BLOCK, PEER_HEADS, PEER_TOPK * PEER_TOPK)
        best_s, best_j = lax.top_k(cand_s, PEER_TOPK)
        idx = jnp.take_along_axis(cand_i, best_j, axis=-1)
        g = jax.nn.softmax(best_s, axis=-1)
        u = u_tab[idx]
        act = jax.nn.gelu(jnp.einsum("td,thkd->thk", xb, u).astype(jnp.float32))
        v = v_tab[idx]
        return jnp.einsum("thk,thkd->td", (g * act).astype(xb.dtype), v)

    return lax.map(block, xt).reshape(bsz, seq_len, dm)


def trunk_layer(x, cond, h0, p):
    mod = jax.nn.silu(cond) @ p["w_ada"] + p["b_ada"]
    sh1, sc1, g1, sh2, sc2, g2 = jnp.split(mod[:, None, :], 6, axis=-1)
    y, final_state = token_mixer(x * (1 + sc1) + sh1, h0, p)
    x = layer_norm(DN_ALPHA * x + g1 * y, p["ln1_g"], p["ln1_b"])
    y = peer(x * (1 + sc2) + sh2, p["peer_w_query"], p["peer_sub_keys"], p["peer_u"], p["peer_v"])
    x = layer_norm(DN_ALPHA * x + g2 * y, p["ln2_g"], p["ln2_b"])
    return x, final_state


def setup_inputs(seed: int = 0) -> dict:
    key = jax.random.key(seed)
    ks = iter(jax.random.split(key, 40))
    f32 = jnp.float32

    def nrm(shape, scale):
        return jax.random.normal(next(ks), shape, f32) * scale

    a0 = jax.random.uniform(next(ks), (DEPTH, 2, D_RNN), f32, 0.9, 0.999)
    s0 = a0 ** (1.0 / RGLRU_C)
    rnn_lambda = jnp.log(s0) - jnp.log1p(-s0)
    return {
        "x_prompt": nrm((BATCH, SEQ, D_MODEL), 1.0),
        "x_sample": nrm((DEC_BATCH, DEC_SEQ, D_MODEL), 1.0),
        "state_rglru": nrm((DEC_BATCH, DEPTH, 2, D_RNN), 0.5),
        "c": nrm((DEC_BATCH, D_MODEL), 1.0),
        "c_ctx": nrm((D_MODEL,), 1.0),
        "w_ada": nrm((DEPTH, D_MODEL, 6 * D_MODEL), 0.5 * D_MODEL ** -0.5),
        "b_ada": nrm((DEPTH, 6 * D_MODEL), 0.02),
        "w_in": nrm((DEPTH, D_MODEL, D_IN), D_MODEL ** -0.5),
        "rnn_conv_w": nrm((DEPTH, RNN_CONV_W, D_RNN), RNN_CONV_W ** -0.5),
        "rnn_conv_b": nrm((DEPTH, D_RNN), 0.02),
        "rnn_gate_w": nrm((DEPTH, 2, 2, N_RNN_HEADS, RNN_HEAD_DIM, RNN_HEAD_DIM), RNN_HEAD_DIM ** -0.5),
        "rnn_gate_b": nrm((DEPTH, 2, 2, D_RNN), 0.02),
        "rnn_lambda": rnn_lambda,
        "hy_conv_w": nrm((DEPTH, HY_CONV_W, 3 * D_HY), HY_CONV_W ** -0.5),
        "hy_conv_b": nrm((DEPTH, 3 * D_HY), 0.02),
        "hy_ffn_w1": nrm((DEPTH, HY_EMB_DIM, HY_FILTER_HIDDEN), HY_EMB_DIM ** -0.5),
        "hy_ffn_b1": nrm((DEPTH, HY_FILTER_HIDDEN), 0.1),
        "hy_ffn_w2": nrm((DEPTH, HY_FILTER_HIDDEN, HY_FILTER_HIDDEN), HY_FILTER_HIDDEN ** -0.5),
        "hy_ffn_b2": nrm((DEPTH, HY_FILTER_HIDDEN), 0.1),
        "hy_ffn_w3": nrm((DEPTH, HY_FILTER_HIDDEN, 2 * HY_ORDER * D_HY), 0.05 * HY_FILTER_HIDDEN ** -0.5),
        "hy_ffn_b3": nrm((DEPTH, 2 * HY_ORDER * D_HY), 0.005),
        "hy_sin_freq": 1.0 + nrm((DEPTH, HY_FILTER_HIDDEN), 0.05),
        "hy_skip": nrm((DEPTH, HY_ORDER, D_HY), 0.5),
        "w_branch_a": nrm((DEPTH, D_RNN, D_MODEL), D_RNN ** -0.5),
        "w_branch_b": nrm((DEPTH, D_HY, D_MODEL), D_HY ** -0.5),
        "w_out": nrm((DEPTH, D_MODEL, D_MODEL), DN_BETA * D_MODEL ** -0.5),
        "ln1_g": 1.0 + nrm((DEPTH, D_MODEL), 0.02),
        "ln1_b": nrm((DEPTH, D_MODEL), 0.02),
        "ln2_g": 1.0 + nrm((DEPTH, D_MODEL), 0.02),
        "ln2_b": nrm((DEPTH, D_MODEL), 0.02),
        "peer_w_query": nrm((DEPTH, D_MODEL, PEER_HEADS * PEER_KEY_DIM), D_MODEL ** -0.5),
        "peer_sub_keys": nrm((DEPTH, 2, N_KEYS, PEER_HALF), PEER_HALF ** -0.5),
        "peer_u": nrm((DEPTH, N_EXPERTS, D_MODEL), D_MODEL ** -0.5),
        "peer_v": nrm((DEPTH, N_EXPERTS, D_MODEL), DN_BETA * PEER_HEADS ** -0.5),
    }


def reference(x_prompt, x_sample, state_rglru, c, c_ctx, w_ada, b_ada, w_in, rnn_conv_w, rnn_conv_b,
              rnn_gate_w, rnn_gate_b, rnn_lambda, hy_conv_w, hy_conv_b, hy_ffn_w1, hy_ffn_b1, hy_ffn_w2,
              hy_ffn_b2, hy_ffn_w3, hy_ffn_b3, hy_sin_freq, hy_skip, w_branch_a, w_branch_b, w_out,
              ln1_g, ln1_b, ln2_g, ln2_b, peer_w_query, peer_sub_keys, peer_u, peer_v):
    def params_of(l):
        return {
            "w_ada": w_ada[l], "b_ada": b_ada[l], "w_in": w_in[l],
            "rnn_conv_w": rnn_conv_w[l], "rnn_conv_b": rnn_conv_b[l],
            "rnn_gate_w": rnn_gate_w[l], "rnn_gate_b": rnn_gate_b[l], "rnn_lambda": rnn_lambda[l],
            "hy_conv_w": hy_conv_w[l], "hy_conv_b": hy_conv_b[l],
            "hy_ffn_w1": hy_ffn_w1[l], "hy_ffn_b1": hy_ffn_b1[l], "hy_ffn_w2": hy_ffn_w2[l],
            "hy_ffn_b2": hy_ffn_b2[l], "hy_ffn_w3": hy_ffn_w3[l], "hy_ffn_b3": hy_ffn_b3[l],
            "hy_sin_freq": hy_sin_freq[l], "hy_skip": hy_skip[l],
            "w_branch_a": w_branch_a[l], "w_branch_b": w_branch_b[l], "w_out": w_out[l],
            "ln1_g": ln1_g[l], "ln1_b": ln1_b[l], "ln2_g": ln2_g[l], "ln2_b": ln2_b[l],
            "peer_w_query": peer_w_query[l], "peer_sub_keys": peer_sub_keys[l],
            "peer_u": peer_u[l], "peer_v": peer_v[l],
        }

    xp = x_prompt
    cond_ctx = jnp.broadcast_to(c_ctx, (x_prompt.shape[0], D_MODEL))
    ctx_states = []
    for l in range(DEPTH):
        h0 = jnp.zeros((x_prompt.shape[0], 2, D_RNN), jnp.float32)
        xp, st = trunk_layer(xp, cond_ctx, h0, params_of(l))
        ctx_states.append(st.astype(x_prompt.dtype))
    new_state_rglru = jnp.stack(ctx_states, axis=1)

    xs = x_sample + grid_pos_embed(x_sample.shape[1]).astype(x_sample.dtype)[None]
    for l in range(DEPTH):
        xs, _ = trunk_layer(xs, c, state_rglru[:, l], params_of(l))

    return (xp, xs, new_state_rglru)
```

```python
import functools
import math

import numpy as np
import jax
import jax.numpy as jnp
from jax import lax
from jax.experimental import pallas as pl
from jax.experimental.pallas import tpu as pltpu

F32 = jnp.float32
BF16 = jnp.bfloat16

D_MODEL = 1024
D_RNN = 1024
N_RNN_HEADS = 4
RNN_HEAD_DIM = D_RNN // N_RNN_HEADS
RNN_CONV_W = 4
RGLRU_C = 8.0
D_HY = 1024
HY_ORDER = 2
HY_EMB_BANDS = 16
HY_EMB_DIM = 1 + 2 * HY_EMB_BANDS
HY_EMB_PAD = 64
HY_FILTER_HIDDEN = 64
HY_DECAY_TARGET = 1e-2
HY_MIN_DECAY = math.log(HY_DECAY_TARGET) / 1.5
HY_MAX_DECAY = math.log(HY_DECAY_TARGET) / 0.3
GRID_W = 64
N_KEYS = 128
N_EXPERTS = N_KEYS * N_KEYS
PEER_HEADS = 8
PEER_HALF = 128
PEER_TOPK = 16
DEPTH = 1
DN_ALPHA = (2.0 * DEPTH) ** 0.25
LN_EPS = 1e-5

MOD_ROWS = 16
MIB = 1024 * 1024
NEG_INF = float("-inf")


def _params(semantics, vmem_mib):
    return pltpu.CompilerParams(dimension_semantics=semantics, vmem_limit_bytes=vmem_mib * MIB)


def _gelu(x):
    return jax.nn.gelu(x, approximate=True)


def _dot(a, b):
    return jnp.dot(a, b, preferred_element_type=F32)


def _dot_f32(a, b):
    return jnp.dot(a, b, preferred_element_type=F32, precision=lax.Precision.HIGHEST)


def _ada_body(c_ref, w_ref, b_ref, o_ref):
    c = c_ref[...]
    o_ref[...] = _dot_f32(c * jax.nn.sigmoid(c), w_ref[...]) + b_ref[...]


def _ada(cond, w_ada, b_ada):
    n = w_ada.shape[1]
    tn = 1024
    return pl.pallas_call(
        _ada_body,
        grid=(n // tn,),
        in_specs=[pl.BlockSpec((MOD_ROWS, D_MODEL), lambda j: (0, 0)),
                  pl.BlockSpec((D_MODEL, tn), lambda j: (0, j)),
                  pl.BlockSpec((1, tn), lambda j: (0, j))],
        out_specs=pl.BlockSpec((MOD_ROWS, tn), lambda j: (0, j)),
        out_shape=jax.ShapeDtypeStruct((MOD_ROWS, n), F32),
        compiler_params=_params(("arbitrary",), 32),
        name="ada",
    )(cond, w_ada, b_ada)


def _inproj_body(*refs, has_pos):
    if has_pos:
        x_ref, pos_ref, mod_ref, w_ref, rx_ref, gg_ref, hy_ref, ga_ref, gb_ref = refs
    else:
        x_ref, mod_ref, w_ref, rx_ref, gg_ref, hy_ref, ga_ref, gb_ref = refs
    x = x_ref[0]
    if has_pos:
        x = x + pos_ref[...]
    mod = mod_ref[0]
    sh1 = mod[:, 0:D_MODEL]
    sc1 = mod[:, D_MODEL:2 * D_MODEL]
    h = (x * (1.0 + sc1) + sh1).astype(BF16)
    o = 0
    rx_ref[...] = _dot(h, w_ref[:, o:o + D_RNN])
    o += D_RNN
    gg_ref[0] = _gelu(_dot(h, w_ref[:, o:o + D_RNN])).astype(BF16)
    o += D_RNN
    for j in range(3):
        hy_ref[0, :, j * D_HY:(j + 1) * D_HY] = _dot(h, w_ref[:, o:o + D_HY])
        o += D_HY
    ga_ref[0] = jax.nn.sigmoid(_dot(h, w_ref[:, o:o + D_MODEL])).astype(BF16)
    o += D_MODEL
    gb_ref[0] = jax.nn.sigmoid(_dot(h, w_ref[:, o:o + D_MODEL])).astype(BF16)


def _inproj(x, pos, mod3, w_in_bf, mod_row_of_batch, tl):
    b, l, _ = x.shape
    d_in = w_in_bf.shape[1]
    has_pos = pos is not None
    in_specs = [pl.BlockSpec((1, tl, D_MODEL), lambda bi, ti: (bi, ti, 0))]
    args = [x]
    if has_pos:
        in_specs.append(pl.BlockSpec((tl, D_MODEL), lambda bi, ti: (ti, 0)))
        args.append(pos)
    in_specs += [pl.BlockSpec((1, 1, 6 * D_MODEL), lambda bi, ti: (mod_row_of_batch(bi), 0, 0)),
                 pl.BlockSpec((D_MODEL, d_in), lambda bi, ti: (0, 0), pipeline_mode=pl.Buffered(1))]
    args += [mod3, w_in_bf]
    tok = lambda bi, ti: (bi, ti, 0)
    out_specs = [pl.BlockSpec((tl, D_RNN), lambda bi, ti: (ti, bi)),
                 pl.BlockSpec((1, tl, D_RNN), tok),
                 pl.BlockSpec((1, tl, 3 * D_HY), tok),
                 pl.BlockSpec((1, tl, D_MODEL), tok),
                 pl.BlockSpec((1, tl, D_MODEL), tok)]
    out_shape = [jax.ShapeDtypeStruct((l, b * D_RNN), F32),
                 jax.ShapeDtypeStruct((b, l, D_RNN), BF16),
                 jax.ShapeDtypeStruct((b, l, 3 * D_HY), F32),
                 jax.ShapeDtypeStruct((b, l, D_MODEL), BF16),
                 jax.ShapeDtypeStruct((b, l, D_MODEL), BF16)]
    return pl.pallas_call(
        functools.partial(_inproj_body, has_pos=has_pos),
        grid=(b, l // tl),
        in_specs=in_specs, out_specs=out_specs, out_shape=out_shape,
        compiler_params=_params(("parallel", "parallel"), 56),
        name="inproj",
    )(*args)


def _softplus(x):
    return jnp.maximum(x, 0.0) + jnp.log(1.0 + jnp.exp(-jnp.abs(x)))


def _rglru_body(rx_ref, h0_ref, wg_ref, bg_ref, lam_ref, cw_ref, cb_ref, y_ref, st_ref, a_s, u_s,
                *, seq, bb, tc):
    hd = RNN_HEAD_DIM
    nchunks = seq // tc
    cw = cw_ref[...]
    cb = cb_ref[...]
    for d in range(2):
        coef = -RGLRU_C * _softplus(-lam_ref[0, d:d + 1, :])
        wd = wg_ref[0, :, 2 * d * hd:2 * (d + 1) * hd]
        bd = bg_ref[0, :, 2 * d * hd:2 * (d + 1) * hd]

        def chunk(ci, h, d=d, coef=coef, wd=wd, bd=bd):
            c = ci if d == 0 else nchunks - 1 - ci
            t0 = pl.multiple_of(c * tc, tc)
            lo_ok = (c > 0).astype(F32)
            hi_ok = (c < nchunks - 1).astype(F32)
            lo = rx_ref[pl.ds(jnp.maximum(t0 - 2, 0), 2)] * lo_ok
            hi = rx_ref[pl.ds(jnp.minimum(t0 + tc, seq - 1), 1)] * hi_ok
            xe = jnp.concatenate([lo, rx_ref[pl.ds(t0, tc)], hi], axis=0)
            xc = cb[None] + sum(cw[k:k + 1][None] * xe[k:k + tc] for k in range(RNN_CONV_W))
            xc2 = xc.reshape(tc * bb, hd)
            g = _dot(xc2.astype(BF16), wd) + bd
            r = jax.nn.sigmoid(g[:, :hd])
            i = jax.nn.sigmoid(g[:, hd:])
            a = jnp.exp(coef * r)
            u = jnp.sqrt(1.0 - a * a) * (i * xc2)
            a_s[...] = a.reshape(tc, bb, hd)
            u_s[...] = u.reshape(tc, bb, hd)

            def step(j, h):
                tau = j if d == 0 else tc - 1 - j
                h = a_s[tau] * h + u_s[tau]
                if d == 0:
                    y_ref[t0 + tau] = h
                else:
                    y_ref[t0 + tau] = y_ref[t0 + tau] + h
                return h

            return lax.fori_loop(0, tc, step, h, unroll=8)

        h_fin = lax.fori_loop(0, nchunks, chunk, h0_ref[d])
        st_ref[d] = h_fin


def _rglru(rx3, h0, wg, bg, lam, cw, cb, bb):
    seq, b, _ = rx3.shape
    hd = RNN_HEAD_DIM
    tc = max(8, 512 // bb)
    blk = lambda bi, hi: (0, bi, hi)
    return pl.pallas_call(
        functools.partial(_rglru_body, seq=seq, bb=bb, tc=tc),
        grid=(b // bb, N_RNN_HEADS),
        in_specs=[pl.BlockSpec((seq, bb, hd), blk),
                  pl.BlockSpec((2, bb, hd), blk),
                  pl.BlockSpec((1, hd, 4 * hd), lambda bi, hi: (hi, 0, 0)),
                  pl.BlockSpec((1, 1, 4 * hd), lambda bi, hi: (hi, 0, 0)),
                  pl.BlockSpec((1, 2, hd), lambda bi, hi: (hi, 0, 0)),
                  pl.BlockSpec((RNN_CONV_W, hd), lambda bi, hi: (0, hi)),
                  pl.BlockSpec((1, hd), lambda bi, hi: (0, hi))],
        out_specs=[pl.BlockSpec((seq, bb, hd), blk),
                   pl.BlockSpec((2, bb, hd), blk)],
        out_shape=[jax.ShapeDtypeStruct((seq, b, D_RNN), F32),
                   jax.ShapeDtypeStruct((2, b, D_RNN), F32)],
        scratch_shapes=[pltpu.VMEM((tc, bb, hd), F32), pltpu.VMEM((tc, bb, hd), F32)],
        compiler_params=_params(("parallel", "parallel"), 56),
        name="rglru",
    )(rx3, h0, wg, bg, lam, cw, cb)


def _dft_matrices(seq):
    k = np.arange(seq, dtype=np.float64)[:, None]
    s = np.arange(seq, dtype=np.float64)[None, :]
    ang = np.pi * k * s / seq
    top = np.cos(ang)
    bot = -np.sin(ang)
    bot[0, :] = np.where(np.arange(seq) % 2 == 0, 1.0, -1.0)
    fwd = np.concatenate([top, bot], axis=0)
    return jnp.asarray(fwd, dtype=BF16), jnp.asarray(fwd.T, dtype=BF16)


def _filter_features(seq):
    t = np.arange(seq, dtype=np.float32)
    t_norm = t / np.float32(max(seq - 1, 1))
    w = (np.float32(2.0 * math.pi) * t / np.float32(seq)).astype(np.float32)
    bands = np.linspace(1e-4, HY_EMB_BANDS - 1, HY_EMB_BANDS, dtype=np.float32)
    fw = w[:, None] * bands[None, :]
    z = np.concatenate([t_norm[:, None], np.cos(fw), -np.sin(fw)], axis=-1).astype(np.float32)
    zp = np.zeros((seq, HY_EMB_PAD), np.float32)
    zp[:, :HY_EMB_DIM] = z
    deltas = np.abs(np.linspace(HY_MIN_DECAY, HY_MAX_DECAY, D_HY, dtype=np.float32))
    return jnp.asarray(zp), jnp.asarray(t_norm[:, None]), jnp.asarray(deltas[None, :])


def _dot_split(f_bf, x):
    hi = x.astype(BF16)
    lo = (x - hi.astype(F32)).astype(BF16)
    return _dot(f_bf, hi) + _dot(f_bf, lo)


def _hyfilt_body(z_ref, tn_ref, dl_ref, w1_ref, b1_ref, w2_ref, b2_ref, fr_ref, w3f_ref, b3f_ref,
                 w3b_ref, b3b_ref, f_ref, kr_ref, ki_ref, *, seq):
    freq = fr_ref[...]
    hid = jnp.sin(freq * (_dot_f32(z_ref[...], w1_ref[...]) + b1_ref[...]))
    hid = jnp.sin(freq * (_dot_f32(hid, w2_ref[...]) + b2_ref[...]))
    decay = jnp.exp(-tn_ref[...] * dl_ref[...])
    ff = (_dot_f32(hid, w3f_ref[...]) + b3f_ref[...]) * decay
    fb = (_dot_f32(hid, w3b_ref[...]) + b3b_ref[...]) * decay
    row = lax.broadcasted_iota(jnp.int32, ff.shape, 0)
    fb = jnp.where(row == 0, 0.0, fb)
    s = ff + fb
    dm = ff - fb
    sign = jnp.where((row & 1) == 0, 1.0, -1.0)
    nyq = jnp.sum(s * sign, axis=0, keepdims=True)
    kr = _dot_split(f_ref[0:seq, :], s)
    ki = _dot_split(f_ref[seq:2 * seq, :], dm)
    scale = jnp.where(row == 0, 0.5 / seq, 1.0 / seq)
    kr_ref[0] = kr * scale
    ki_ref[0] = jnp.where(row == 0, nyq, ki) * scale


def _hyfilt(seq, fwd, p):
    z, tn, dl = _filter_features(seq)
    dt = 512
    nd = D_HY // dt
    hidn = HY_FILTER_HIDDEN
    const = lambda o, j: (0, 0)
    w3 = p["hy_ffn_w3"]
    b3 = p["hy_ffn_b3"][None, :]
    in_specs = [pl.BlockSpec((seq, HY_EMB_PAD), const),
                pl.BlockSpec((seq, 1), const),
                pl.BlockSpec((1, dt), lambda o, j: (0, j)),
                pl.BlockSpec((HY_EMB_PAD, hidn), const),
                pl.BlockSpec((1, hidn), const),
                pl.BlockSpec((hidn, hidn), const),
                pl.BlockSpec((1, hidn), const),
                pl.BlockSpec((1, hidn), const),
                pl.BlockSpec((hidn, dt), lambda o, j: (0, o * nd + j)),
                pl.BlockSpec((1, dt), lambda o, j: (0, o * nd + j)),
                pl.BlockSpec((hidn, dt), lambda o, j: (0, (HY_ORDER + o) * nd + j)),
                pl.BlockSpec((1, dt), lambda o, j: (0, (HY_ORDER + o) * nd + j)),
                pl.BlockSpec((2 * seq, seq), const)]
    w1p = jnp.zeros((HY_EMB_PAD, hidn), F32).at[:HY_EMB_DIM].set(p["hy_ffn_w1"])
    out_spec = pl.BlockSpec((1, seq, dt), lambda o, j: (o, 0, j))
    return pl.pallas_call(
        functools.partial(_hyfilt_body, seq=seq),
        grid=(HY_ORDER, nd),
        in_specs=in_specs,
        out_specs=[out_spec, out_spec],
        out_shape=[jax.ShapeDtypeStruct((HY_ORDER, seq, D_HY), F32)] * 2,
        compiler_params=_params(("parallel", "parallel"), 48),
        name="hyfilt",
    )(z, tn, dl, w1p, p["hy_ffn_b1"][None, :], p["hy_ffn_w2"], p["hy_ffn_b2"][None, :],
      p["hy_sin_freq"][None, :], w3, b3, w3, b3, fwd)


def _hyena_body(x1_ref, x2_ref, v_ref, w1_ref, w2_ref, wv_ref, b1_ref, b2_ref, bv_ref,
                kr_ref, ki_ref, sk_ref, f_ref, ft_ref, o_ref, *, seq):
    shape = v_ref.shape[1:]
    row = lax.broadcasted_iota(jnp.int32, shape, 0)
    first = row == 0
    last = row == seq - 1

    def conv3(x_ref, w_ref, b_ref):
        x = x_ref[0]
        xm = jnp.where(first, 0.0, pltpu.roll(x, 1, 0))
        xp = jnp.where(last, 0.0, pltpu.roll(x, seq - 1, 0))
        return w_ref[0:1, :] * xm + w_ref[1:2, :] * x + w_ref[2:3, :] * xp + b_ref[...]

    def long_conv(u, o):
        kr = kr_ref[o]
        ki = ki_ref[o]
        kiz = jnp.where(first, 0.0, ki)
        krb = jnp.where(first, ki, kr)
        uf = _dot(f_ref[...], u.astype(BF16))
        top = uf[:seq]
        bot = uf[seq:]
        y = jnp.concatenate([top * kr - bot * kiz, top * kiz + bot * krb], axis=0).astype(BF16)
        return _dot(ft_ref[...], y) + u * sk_ref[o:o + 1, :]

    z = conv3(x1_ref, w1_ref, b1_ref) * long_conv(conv3(v_ref, wv_ref, bv_ref), 0)
    z = conv3(x2_ref, w2_ref, b2_ref) * long_conv(z, 1)
    o_ref[0] = z.astype(BF16)


def _hyena(hy, cw, cb, kr, ki, skip, fwd, inv, dt):
    b, seq, _ = hy.shape
    nd = D_HY // dt
    const = lambda bi, j: (0, 0)
    part = lambda k: (lambda bi, j: (bi, 0, k * nd + j))
    wpart = lambda k: (lambda bi, j: (0, k * nd + j))
    in_specs = ([pl.BlockSpec((1, seq, dt), part(k)) for k in range(3)]
                + [pl.BlockSpec((3, dt), wpart(k)) for k in range(3)]
                + [pl.BlockSpec((1, dt), wpart(k)) for k in range(3)]
                + [pl.BlockSpec((HY_ORDER, seq, dt), lambda bi, j: (0, 0, j)),
                   pl.BlockSpec((HY_ORDER, seq, dt), lambda bi, j: (0, 0, j)),
                   pl.BlockSpec((HY_ORDER, dt), lambda bi, j: (0, j)),
                   pl.BlockSpec((2 * seq, seq), const, pipeline_mode=pl.Buffered(1)),
                   pl.BlockSpec((seq, 2 * seq), const, pipeline_mode=pl.Buffered(1))])
    return pl.pallas_call(
        functools.partial(_hyena_body, seq=seq),
        grid=(b, nd),
        in_specs=in_specs,
        out_specs=pl.BlockSpec((1, seq, dt), lambda bi, j: (bi, 0, j)),
        out_shape=jax.ShapeDtypeStruct((b, seq, D_HY), BF16),
        compiler_params=_params(("parallel", "parallel"), 56),
        name="hyena",
    )(hy, hy, hy, cw, cw, cw, cb, cb, cb, kr, ki, skip, fwd, inv)


def _layer_norm(x, g, b):
    mu = jnp.mean(x, axis=-1, keepdims=True)
    xc = x - mu
    var = jnp.mean(xc * xc, axis=-1, keepdims=True)
    return xc * lax.rsqrt(var + LN_EPS) * g + b


def _merge_body(*refs, has_pos):
    if has_pos:
        (x_ref, pos_ref, mod_ref, yr_ref, gg_ref, z_ref, ga_ref, gb_ref, wa_ref, wb_ref, wo_ref,
         lg_ref, lb_ref, wq_ref, x1_ref, h2_ref, q_ref) = refs
    else:
        (x_ref, mod_ref, yr_ref, gg_ref, z_ref, ga_ref, gb_ref, wa_ref, wb_ref, wo_ref,
         lg_ref, lb_ref, wq_ref, x1_ref, h2_ref, q_ref) = refs
    d = D_MODEL
    mod = mod_ref[0]
    g1 = mod[:, 2 * d:3 * d]
    sh2 = mod[:, 3 * d:4 * d]
    sc2 = mod[:, 4 * d:5 * d]
    ya = _dot((yr_ref[...].astype(BF16) * gg_ref[0]), wa_ref[...])
    yb = _dot(z_ref[0], wb_ref[...])
    m = ga_ref[0].astype(F32) * ya + gb_ref[0].astype(F32) * yb
    y = _dot(m.astype(BF16), wo_ref[...])
    x = x_ref[0]
    if has_pos:
        x = x + pos_ref[...]
    x1 = _layer_norm(DN_ALPHA * x + g1 * y, lg_ref[...], lb_ref[...])
    x1_ref[0] = x1
    h2 = (x1 * (1.0 + sc2) + sh2).astype(BF16)
    h2_ref[0] = h2
    q_ref[0] = _dot(h2, wq_ref[...]).astype(BF16)


def _merge(x, pos, mod3, mod_row_of_batch, yr2, gg, z, ga, gb, wa, wb, wo, lg, lb, wq, tl):
    b, l, d = x.shape
    nq = wq.shape[1]
    has_pos = pos is not None
    tok = lambda bi, ti: (bi, ti, 0)
    const = lambda bi, ti: (0, 0)
    in_specs = [pl.BlockSpec((1, tl, d), tok)]
    args = [x]
    if has_pos:
        in_specs.append(pl.BlockSpec((tl, d), lambda bi, ti: (ti, 0)))
        args.append(pos)
    in_specs += [pl.BlockSpec((1, 1, 6 * d), lambda bi, ti: (mod_row_of_batch(bi), 0, 0)),
                 pl.BlockSpec((tl, D_RNN), lambda bi, ti: (ti, bi)),
                 pl.BlockSpec((1, tl, d), tok), pl.BlockSpec((1, tl, d), tok),
                 pl.BlockSpec((1, tl, d), tok), pl.BlockSpec((1, tl, d), tok),
                 pl.BlockSpec((d, d), const), pl.BlockSpec((d, d), const), pl.BlockSpec((d, d), const),
                 pl.BlockSpec((1, d), const), pl.BlockSpec((1, d), const),
                 pl.BlockSpec((d, nq), const)]
    args += [mod3, yr2, gg, z, ga, gb, wa, wb, wo, lg, lb, wq]
    return pl.pallas_call(
        functools.partial(_merge_body, has_pos=has_pos),
        grid=(b, l // tl),
        in_specs=in_specs,
        out_specs=[pl.BlockSpec((1, tl, d), tok), pl.BlockSpec((1, tl, d), tok),
                   pl.BlockSpec((1, tl, nq), tok)],
        out_shape=[jax.ShapeDtypeStruct((b, l, d), F32), jax.ShapeDtypeStruct((b, l, d), BF16),
                   jax.ShapeDtypeStruct((b, l, nq), BF16)],
        compiler_params=_params(("parallel", "parallel"), 56),
        name="merge",
    )(*args)


RANK_NONE = 1.0e9


def _topk_rank(s, k):
    n = s.shape[0]
    iota = lax.broadcasted_iota(jnp.int32, s.shape, 0)
    rank = jnp.full(s.shape, RANK_NONE, F32)
    vals = []
    for r in range(k):
        m = jnp.max(s, axis=0, keepdims=True)
        idx = jnp.min(jnp.where(s == m, iota, n), axis=0, keepdims=True)
        hit = iota == idx
        rank = jnp.where(hit, float(r), rank)
        s = jnp.where(hit, NEG_INF, s)
        vals.append(m)
    return vals, rank


def _peersel_body(q_ref, k_ref, e1n_ref, ni_ref, e2_ref, r2_ref):
    k = PEER_TOPK
    nt = (((1,), (1,)), ((), ()))
    s1 = lax.dot_general(k_ref[0], q_ref[:, 0:PEER_HALF], nt, preferred_element_type=F32)
    s2 = lax.dot_general(k_ref[1], q_ref[:, PEER_HALF:2 * PEER_HALF], nt, preferred_element_type=F32)
    v1, rank1 = _topk_rank(s1, k)
    v2, rank2 = _topk_rank(s2, k)
    v2_16 = jnp.concatenate(v2, axis=0)
    v2_8 = v2_16[0:8]
    v1_hi = jnp.concatenate(v1[8:16], axis=0)
    cand = jnp.concatenate([v1[0] + v2_16] + [v1[a] + v2_8 for a in range(1, 8)] + [v1_hi + v2[0]], axis=0)
    _, rankc = _topk_rank(cand, k)
    sel = jnp.where(rankc < float(k), 1.0, 0.0)
    z = jnp.sum(sel * jnp.exp(cand - cand[0:1]), axis=0, keepdims=True)
    counts = [jnp.sum(sel[0:16], axis=0, keepdims=True)]
    counts += [jnp.sum(sel[8 + 8 * a:16 + 8 * a], axis=0, keepdims=True) for a in range(1, 8)]
    counts += [sel[72 + a:73 + a] for a in range(8)]
    ni = jnp.zeros(s1.shape, F32)
    for a in range(k):
        ni = jnp.where(rank1 == float(a), counts[a], ni)
    e1n_ref[0] = jnp.exp(s1 - v1[0]) / z
    ni_ref[0] = ni
    e2_ref[0] = jnp.exp(s2 - v2[0])
    r2_ref[0] = rank2


def _peersel(q2, keys_bf, tt):
    t = q2.shape[0]
    out_spec = pl.BlockSpec((1, N_KEYS, tt), lambda ti, h: (h, 0, ti))
    return pl.pallas_call(
        _peersel_body,
        grid=(t // tt, PEER_HEADS),
        in_specs=[pl.BlockSpec((tt, 2 * PEER_HALF), lambda ti, h: (ti, h)),
                  pl.BlockSpec((2, N_KEYS, PEER_HALF), lambda ti, h: (0, 0, 0))],
        out_specs=[out_spec] * 4,
        out_shape=[jax.ShapeDtypeStruct((PEER_HEADS, N_KEYS, t), F32)] * 4,
        compiler_params=_params(("parallel", "parallel"), 48),
        name="peersel",
    )(q2, keys_bf)


PEER_EB = 1024


def _peermix_body(h2_ref, u_ref, vt_ref, e1n_ref, ni_ref, e2_ref, r2_ref, x1_ref, mod_ref, lg_ref, lb_ref,
                  o_ref, acc_ref, w_ref):
    e = pl.program_id(1)

    @pl.when(e == 0)
    def _():
        acc_ref[...] = jnp.zeros(acc_ref.shape, F32)

    nt = (((1,), (1,)), ((), ()))
    act = lax.dot_general(u_ref[...], h2_ref[...], nt, preferred_element_type=F32)
    for s in range(PEER_EB // N_KEYS):
        g = None
        for h in range(PEER_HEADS):
            nrow = ni_ref[h, s:s + 1, :]
            erow = e1n_ref[h, s:s + 1, :]
            term = jnp.where(r2_ref[h] < nrow, e2_ref[h], 0.0) * erow
            g = term if g is None else g + term
        w_ref[s * N_KEYS:(s + 1) * N_KEYS, :] = (_gelu(act[s * N_KEYS:(s + 1) * N_KEYS]) * g).astype(BF16)
    acc_ref[...] += _dot(vt_ref[...], w_ref[...])

    @pl.when(e == pl.num_programs(1) - 1)
    def _():
        d = D_MODEL
        g2 = mod_ref[0][:, 5 * d:6 * d]
        y = acc_ref[...].T
        o_ref[...] = _layer_norm(DN_ALPHA * x1_ref[...] + g2 * y, lg_ref[...], lb_ref[...])


def _peermix(h2, u_bf, vt_bf, e1n, ni, e2, r2, x1, mod3, mod_row_of_tile, lg, lb, tt):
    t, d = h2.shape
    ne = u_bf.shape[0]
    ns = PEER_EB // N_KEYS
    tile = lambda ti, e: (ti, 0)
    const = lambda ti, e: (0, 0)
    return pl.pallas_call(
        _peermix_body,
        grid=(t // tt, ne // PEER_EB),
        in_specs=[pl.BlockSpec((tt, d), tile),
                  pl.BlockSpec((PEER_EB, d), lambda ti, e: (e, 0)),
                  pl.BlockSpec((d, PEER_EB), lambda ti, e: (0, e)),
                  pl.BlockSpec((PEER_HEADS, ns, tt), lambda ti, e: (0, e, ti)),
                  pl.BlockSpec((PEER_HEADS, ns, tt), lambda ti, e: (0, e, ti)),
                  pl.BlockSpec((PEER_HEADS, N_KEYS, tt), lambda ti, e: (0, 0, ti)),
                  pl.BlockSpec((PEER_HEADS, N_KEYS, tt), lambda ti, e: (0, 0, ti)),
                  pl.BlockSpec((tt, d), tile),
                  pl.BlockSpec((1, 1, 6 * d), lambda ti, e: (mod_row_of_tile(ti), 0, 0)),
                  pl.BlockSpec((1, d), const), pl.BlockSpec((1, d), const)],
        out_specs=pl.BlockSpec((tt, d), tile),
        out_shape=jax.ShapeDtypeStruct((t, d), F32),
        scratch_shapes=[pltpu.VMEM((d, tt), F32), pltpu.VMEM((PEER_EB, tt), BF16)],
        compiler_params=_params(("parallel", "arbitrary"), 56),
        name="peermix",
    )(h2, u_bf, vt_bf, e1n, ni, e2, r2, x1, mod3, lg, lb)


def _prep_params(p):
    hd = RNN_HEAD_DIM
    q = dict(p)
    q["w_in_bf"] = p["w_in"].astype(BF16)
    q["wg"] = jnp.transpose(p["rnn_gate_w"], (2, 3, 0, 1, 4)).reshape(N_RNN_HEADS, hd, 4 * hd).astype(BF16)
    q["bg"] = jnp.transpose(p["rnn_gate_b"].reshape(2, 2, N_RNN_HEADS, hd), (2, 0, 1, 3)).reshape(
        N_RNN_HEADS, 1, 4 * hd)
    q["lam"] = jnp.transpose(p["rnn_lambda"].reshape(2, N_RNN_HEADS, hd), (1, 0, 2))
    q["rnn_cb"] = p["rnn_conv_b"][None, :]
    q["hy_cb"] = p["hy_conv_b"][None, :]
    q["wa"] = p["w_branch_a"].astype(BF16)
    q["wb"] = p["w_branch_b"].astype(BF16)
    q["wo"] = p["w_out"].astype(BF16)
    q["wq"] = p["peer_w_query"].astype(BF16)
    q["ln1_g2"] = p["ln1_g"][None, :]
    q["ln1_b2"] = p["ln1_b"][None, :]
    q["ln2_g2"] = p["ln2_g"][None, :]
    q["ln2_b2"] = p["ln2_b"][None, :]
    return q


def _mixer_group(x, pos, mod3, mod_row_of_batch, h0, q, tl, bb, dt):
    b, l, _ = x.shape
    fwd, inv = _dft_matrices(l)
    kr, ki = _hyfilt(l, fwd, q)
    rx2, gg, hy, ga, gb = _inproj(x, pos, mod3, q["w_in_bf"], mod_row_of_batch, tl)
    y3, st = _rglru(rx2.reshape(l, b, D_RNN), h0, q["wg"], q["bg"], q["lam"], q["rnn_conv_w"],
                    q["rnn_cb"], bb)
    z = _hyena(hy, q["hy_conv_w"], q["hy_cb"], kr, ki, q["hy_skip"], fwd, inv, dt)
    x1, h2, qq = _merge(x, pos, mod3, mod_row_of_batch, y3.reshape(l, b * D_RNN), gg, z, ga, gb,
                        q["wa"], q["wb"], q["wo"], q["ln1_g2"], q["ln1_b2"], q["wq"], tl)
    return x1, h2, qq, st


def _peer_group(x1, h2, qq, mod3, mod_row_of_tile, q, tt_sel, tt_mix):
    b, l, d = x1.shape
    t = b * l
    e1n, ni, e2, r2 = _peersel(qq.reshape(t, qq.shape[-1]), q["keys_bf"], tt_sel)
    out = _peermix(h2.reshape(t, d), q["u_bf"], q["vt_bf"], e1n, ni, e2, r2, x1.reshape(t, d), mod3,
                   mod_row_of_tile, q["ln2_g2"], q["ln2_b2"], tt_mix)
    return out.reshape(b, l, d)


def _grid_pos_embed(n_tokens):
    rows = n_tokens // GRID_W
    t = np.arange(rows * GRID_W)
    r = (t // GRID_W).astype(np.float32)
    col = (t % GRID_W).astype(np.float32)
    quarter = D_MODEL // 4
    omega = (1.0 / (10000.0 ** (np.arange(quarter, dtype=np.float32) / np.float32(quarter)))).astype(np.float32)
    er = r[:, None] * omega[None, :]
    ec = col[:, None] * omega[None, :]
    return jnp.asarray(np.concatenate([np.sin(er), np.cos(er), np.sin(ec), np.cos(ec)], axis=-1), dtype=F32)


PARAM_NAMES = ("w_ada", "b_ada", "w_in", "rnn_conv_w", "rnn_conv_b", "rnn_gate_w", "rnn_gate_b", "rnn_lambda",
               "hy_conv_w", "hy_conv_b", "hy_ffn_w1", "hy_ffn_b1", "hy_ffn_w2", "hy_ffn_b2", "hy_ffn_w3",
               "hy_ffn_b3", "hy_sin_freq", "hy_skip", "w_branch_a", "w_branch_b", "w_out", "ln1_g", "ln1_b",
               "ln2_g", "ln2_b", "peer_w_query", "peer_sub_keys", "peer_u", "peer_v")


def kernel(x_prompt, x_sample, state_rglru, c, c_ctx, w_ada, b_ada, w_in, rnn_conv_w, rnn_conv_b, rnn_gate_w,
           rnn_gate_b, rnn_lambda, hy_conv_w, hy_conv_b, hy_ffn_w1, hy_ffn_b1, hy_ffn_w2, hy_ffn_b2, hy_ffn_w3,
           hy_ffn_b3, hy_sin_freq, hy_skip, w_branch_a, w_branch_b, w_out, ln1_g, ln1_b, ln2_g, ln2_b,
           peer_w_query, peer_sub_keys, peer_u, peer_v):
    stacked = dict(zip(PARAM_NAMES, (w_ada, b_ada, w_in, rnn_conv_w, rnn_conv_b, rnn_gate_w, rnn_gate_b,
                                     rnn_lambda, hy_conv_w, hy_conv_b, hy_ffn_w1, hy_ffn_b1, hy_ffn_w2,
                                     hy_ffn_b2, hy_ffn_w3, hy_ffn_b3, hy_sin_freq, hy_skip, w_branch_a,
                                     w_branch_b, w_out, ln1_g, ln1_b, ln2_g, ln2_b, peer_w_query,
                                     peer_sub_keys, peer_u, peer_v)))
    depth = w_ada.shape[0]
    bp, lp, d = x_prompt.shape
    bs, ls, _ = x_sample.shape
    assert bs + 1 <= MOD_ROWS
    cond = jnp.zeros((MOD_ROWS, d), F32).at[0].set(c_ctx).at[1:1 + bs].set(c)
    pos = _grid_pos_embed(ls)
    tt_mix = 512
    sample_tiles_per_batch = ls // tt_mix

    xp, xs = x_prompt, x_sample
    ctx_states = []
    for layer in range(depth):
        q = _prep_params({name: w[layer] for name, w in stacked.items()})
        q["keys_bf"] = q["peer_sub_keys"].astype(BF16)
        q["u_bf"] = q["peer_u"].astype(BF16)
        q["vt_bf"] = q["peer_v"].T.astype(BF16)
        mod3 = _ada(cond, q["w_ada"], q["b_ada"][None, :]).reshape(MOD_ROWS, 1, 6 * d)

        x1, h2, qq, st = _mixer_group(xp, None, mod3, lambda bi: 0, jnp.zeros((2, bp, D_RNN), F32), q,
                                      tl=lp, bb=32, dt=1024)
        xp = _peer_group(x1, h2, qq, mod3, lambda ti: 0, q, 256, tt_mix)
        ctx_states.append(jnp.transpose(st, (1, 0, 2)))

        pos_l = pos if layer == 0 else None
        h0 = jnp.transpose(state_rglru[:, layer], (1, 0, 2))
        x1, h2, qq, _ = _mixer_group(xs, pos_l, mod3, lambda bi: bi + 1, h0, q, tl=512, bb=8, dt=256)
        xs = _peer_group(x1, h2, qq, mod3, lambda ti: 1 + ti // sample_tiles_per_batch, q, 256, tt_mix)

    new_state = jnp.stack(ctx_states, axis=1).astype(x_prompt.dtype)
    return (xp, xs, new_state)
```

```python
import functools
import math

import numpy as np
import jax
import jax.numpy as jnp
from jax import lax
from jax.experimental import pallas as pl
from jax.experimental.pallas import tpu as pltpu

F32 = jnp.float32
BF16 = jnp.bfloat16

D_MODEL = 1024
D_RNN = 1024
N_RNN_HEADS = 4
RNN_HEAD_DIM = D_RNN // N_RNN_HEADS
RNN_CONV_W = 4
RGLRU_C = 8.0
D_HY = 1024
HY_ORDER = 2
HY_EMB_BANDS = 16
HY_EMB_DIM = 1 + 2 * HY_EMB_BANDS
HY_EMB_PAD = 64
HY_FILTER_HIDDEN = 64
HY_DECAY_TARGET = 1e-2
HY_MIN_DECAY = math.log(HY_DECAY_TARGET) / 1.5
HY_MAX_DECAY = math.log(HY_DECAY_TARGET) / 0.3
GRID_W = 64
N_KEYS = 128
N_EXPERTS = N_KEYS * N_KEYS
PEER_HEADS = 8
PEER_HALF = 128
PEER_TOPK = 16
DEPTH = 1
DN_ALPHA = (2.0 * DEPTH) ** 0.25
LN_EPS = 1e-5

MOD_ROWS = 16
MIB = 1024 * 1024
NEG_INF = float("-inf")


def _params(semantics, vmem_mib):
    return pltpu.CompilerParams(dimension_semantics=semantics, vmem_limit_bytes=vmem_mib * MIB)


def _gelu(x):
    return jax.nn.gelu(x, approximate=True)


def _dot(a, b):
    return jnp.dot(a, b, preferred_element_type=F32)


def _dot_f32(a, b):
    return jnp.dot(a, b, preferred_element_type=F32, precision=lax.Precision.HIGHEST)


def _ada_body(c_ref, w_ref, b_ref, o_ref):
    c = c_ref[...]
    o_ref[...] = _dot_f32(c * jax.nn.sigmoid(c), w_ref[...]) + b_ref[...]


def _ada(cond, w_ada, b_ada):
    n = w_ada.shape[1]
    tn = 1024
    return pl.pallas_call(
        _ada_body,
        grid=(n // tn,),
        in_specs=[pl.BlockSpec((MOD_ROWS, D_MODEL), lambda j: (0, 0)),
                  pl.BlockSpec((D_MODEL, tn), lambda j: (0, j)),
                  pl.BlockSpec((1, tn), lambda j: (0, j))],
        out_specs=pl.BlockSpec((MOD_ROWS, tn), lambda j: (0, j)),
        out_shape=jax.ShapeDtypeStruct((MOD_ROWS, n), F32),
        compiler_params=_params(("arbitrary",), 32),
        name="ada",
    )(cond, w_ada, b_ada)


def _inproj_body(*refs, has_pos):
    if has_pos:
        x_ref, pos_ref, mod_ref, w_ref, rx_ref, gg_ref, hy_ref, ga_ref, gb_ref = refs
    else:
        x_ref, mod_ref, w_ref, rx_ref, gg_ref, hy_ref, ga_ref, gb_ref = refs
    x = x_ref[0]
    if has_pos:
        x = x + pos_ref[...]
    mod = mod_ref[0]
    sh1 = mod[:, 0:D_MODEL]
    sc1 = mod[:, D_MODEL:2 * D_MODEL]
    h = (x * (1.0 + sc1) + sh1).astype(BF16)
    o = 0
    rx_ref[...] = _dot(h, w_ref[:, o:o + D_RNN])
    o += D_RNN
    gg_ref[0] = _gelu(_dot(h, w_ref[:, o:o + D_RNN])).astype(BF16)
    o += D_RNN
    for j in range(3):
        hy_ref[0, :, j * D_HY:(j + 1) * D_HY] = _dot(h, w_ref[:, o:o + D_HY])
        o += D_HY
    ga_ref[0] = jax.nn.sigmoid(_dot(h, w_ref[:, o:o + D_MODEL])).astype(BF16)
    o += D_MODEL
    gb_ref[0] = jax.nn.sigmoid(_dot(h, w_ref[:, o:o + D_MODEL])).astype(BF16)


def _inproj(x, pos, mod3, w_in_bf, mod_row_of_batch, tl):
    b, l, _ = x.shape
    d_in = w_in_bf.shape[1]
    has_pos = pos is not None
    in_specs = [pl.BlockSpec((1, tl, D_MODEL), lambda bi, ti: (bi, ti, 0))]
    args = [x]
    if has_pos:
        in_specs.append(pl.BlockSpec((tl, D_MODEL), lambda bi, ti: (ti, 0)))
        args.append(pos)
    in_specs += [pl.BlockSpec((1, 1, 6 * D_MODEL), lambda bi, ti: (mod_row_of_batch(bi), 0, 0)),
                 pl.BlockSpec((D_MODEL, d_in), lambda bi, ti: (0, 0), pipeline_mode=pl.Buffered(1))]
    args += [mod3, w_in_bf]
    tok = lambda bi, ti: (bi, ti, 0)
    out_specs = [pl.BlockSpec((tl, D_RNN), lambda bi, ti: (ti, bi)),
                 pl.BlockSpec((1, tl, D_RNN), tok),
                 pl.BlockSpec((1, tl, 3 * D_HY), tok),
                 pl.BlockSpec((1, tl, D_MODEL), tok),
                 pl.BlockSpec((1, tl, D_MODEL), tok)]
    out_shape = [jax.ShapeDtypeStruct((l, b * D_RNN), F32),
                 jax.ShapeDtypeStruct((b, l, D_RNN), BF16),
                 jax.ShapeDtypeStruct((b, l, 3 * D_HY), F32),
                 jax.ShapeDtypeStruct((b, l, D_MODEL), BF16),
                 jax.ShapeDtypeStruct((b, l, D_MODEL), BF16)]
    return pl.pallas_call(
        functools.partial(_inproj_body, has_pos=has_pos),
        grid=(b, l // tl),
        in_specs=in_specs, out_specs=out_specs, out_shape=out_shape,
        compiler_params=_params(("parallel", "parallel"), 56),
        name="inproj",
    )(*args)


def _softplus(x):
    return jnp.maximum(x, 0.0) + jnp.log(1.0 + jnp.exp(-jnp.abs(x)))


def _rglru_body(rx_ref, h0_ref, wg_ref, bg_ref, lam_ref, cw_ref, cb_ref, y_ref, st_ref, a_s, u_s,
                *, seq, bb, tc):
    hd = RNN_HEAD_DIM
    nchunks = seq // tc
    cw = cw_ref[...]
    cb = cb_ref[...]
    for d in range(2):
        coef = -RGLRU_C * _softplus(-lam_ref[0, d:d + 1, :])
        wd = wg_ref[0, :, 2 * d * hd:2 * (d + 1) * hd]
        bd = bg_ref[0, :, 2 * d * hd:2 * (d + 1) * hd]

        def chunk(ci, h, d=d, coef=coef, wd=wd, bd=bd):
            c = ci if d == 0 else nchunks - 1 - ci
            t0 = pl.multiple_of(c * tc, tc)
            lo_ok = (c > 0).astype(F32)
            hi_ok = (c < nchunks - 1).astype(F32)
            lo = rx_ref[pl.ds(jnp.maximum(t0 - 2, 0), 2)] * lo_ok
            hi = rx_ref[pl.ds(jnp.minimum(t0 + tc, seq - 1), 1)] * hi_ok
            xe = jnp.concatenate([lo, rx_ref[pl.ds(t0, tc)], hi], axis=0)
            xc = cb[None] + sum(cw[k:k + 1][None] * xe[k:k + tc] for k in range(RNN_CONV_W))
            xc2 = xc.reshape(tc * bb, hd)
            g = _dot(xc2.astype(BF16), wd) + bd
            r = jax.nn.sigmoid(g[:, :hd])
            i = jax.nn.sigmoid(g[:, hd:])
            a = jnp.exp(coef * r)
            u = jnp.sqrt(1.0 - a * a) * (i * xc2)
            a_s[...] = a.reshape(tc, bb, hd)
            u_s[...] = u.reshape(tc, bb, hd)

            def step(j, h):
                tau = j if d == 0 else tc - 1 - j
                h = a_s[tau] * h + u_s[tau]
                if d == 0:
                    y_ref[t0 + tau] = h
                else:
                    y_ref[t0 + tau] = y_ref[t0 + tau] + h
                return h

            return lax.fori_loop(0, tc, step, h, unroll=8)

        h_fin = lax.fori_loop(0, nchunks, chunk, h0_ref[d])
        st_ref[d] = h_fin


def _rglru(rx3, h0, wg, bg, lam, cw, cb, bb):
    seq, b, _ = rx3.shape
    hd = RNN_HEAD_DIM
    tc = max(8, 512 // bb)
    blk = lambda bi, hi: (0, bi, hi)
    return pl.pallas_call(
        functools.partial(_rglru_body, seq=seq, bb=bb, tc=tc),
        grid=(b // bb, N_RNN_HEADS),
        in_specs=[pl.BlockSpec((seq, bb, hd), blk),
                  pl.BlockSpec((2, bb, hd), blk),
                  pl.BlockSpec((1, hd, 4 * hd), lambda bi, hi: (hi, 0, 0)),
                  pl.BlockSpec((1, 1, 4 * hd), lambda bi, hi: (hi, 0, 0)),
                  pl.BlockSpec((1, 2, hd), lambda bi, hi: (hi, 0, 0)),
                  pl.BlockSpec((RNN_CONV_W, hd), lambda bi, hi: (0, hi)),
                  pl.BlockSpec((1, hd), lambda bi, hi: (0, hi))],
        out_specs=[pl.BlockSpec((seq, bb, hd), blk),
                   pl.BlockSpec((2, bb, hd), blk)],
        out_shape=[jax.ShapeDtypeStruct((seq, b, D_RNN), F32),
                   jax.ShapeDtypeStruct((2, b, D_RNN), F32)],
        scratch_shapes=[pltpu.VMEM((tc, bb, hd), F32), pltpu.VMEM((tc, bb, hd), F32)],
        compiler_params=_params(("parallel", "parallel"), 56),
        name="rglru",
    )(rx3, h0, wg, bg, lam, cw, cb)


def _dft_matrices(seq):
    k = np.arange(seq, dtype=np.float64)[:, None]
    s = np.arange(seq, dtype=np.float64)[None, :]
    ang = np.pi * k * s / seq
    top = np.cos(ang)
    bot = -np.sin(ang)
    bot[0, :] = np.where(np.arange(seq) % 2 == 0, 1.0, -1.0)
    fwd = np.concatenate([top, bot], axis=0)
    return jnp.asarray(fwd, dtype=BF16), jnp.asarray(fwd.T, dtype=BF16)


def _filter_features(seq):
    t = np.arange(seq, dtype=np.float32)
    t_norm = t / np.float32(max(seq - 1, 1))
    w = (np.float32(2.0 * math.pi) * t / np.float32(seq)).astype(np.float32)
    bands = np.linspace(1e-4, HY_EMB_BANDS - 1, HY_EMB_BANDS, dtype=np.float32)
    fw = w[:, None] * bands[None, :]
    z = np.concatenate([t_norm[:, None], np.cos(fw), -np.sin(fw)], axis=-1).astype(np.float32)
    zp = np.zeros((seq, HY_EMB_PAD), np.float32)
    zp[:, :HY_EMB_DIM] = z
    deltas = np.abs(np.linspace(HY_MIN_DECAY, HY_MAX_DECAY, D_HY, dtype=np.float32))
    return jnp.asarray(zp), jnp.asarray(t_norm[:, None]), jnp.asarray(deltas[None, :])


def _dot_split(f_bf, x):
    hi = x.astype(BF16)
    lo = (x - hi.astype(F32)).astype(BF16)
    return _dot(f_bf, hi) + _dot(f_bf, lo)


def _hyfilt_body(z_ref, tn_ref, dl_ref, w1_ref, b1_ref, w2_ref, b2_ref, fr_ref, w3f_ref, b3f_ref,
                 w3b_ref, b3b_ref, f_ref, kr_ref, ki_ref, *, seq):
    freq = fr_ref[...]
    hid = jnp.sin(freq * (_dot_f32(z_ref[...], w1_ref[...]) + b1_ref[...]))
    hid = jnp.sin(freq * (_dot_f32(hid, w2_ref[...]) + b2_ref[...]))
    decay = jnp.exp(-tn_ref[...] * dl_ref[...])
    ff = (_dot_f32(hid, w3f_ref[...]) + b3f_ref[...]) * decay
    fb = (_dot_f32(hid, w3b_ref[...]) + b3b_ref[...]) * decay
    row = lax.broadcasted_iota(jnp.int32, ff.shape, 0)
    fb = jnp.where(row == 0, 0.0, fb)
    s = ff + fb
    dm = ff - fb
    sign = jnp.where((row & 1) == 0, 1.0, -1.0)
    nyq = jnp.sum(s * sign, axis=0, keepdims=True)
    kr = _dot_split(f_ref[0:seq, :], s)
    ki = _dot_split(f_ref[seq:2 * seq, :], dm)
    scale = jnp.where(row == 0, 0.5 / seq, 1.0 / seq)
    kr_ref[0] = kr * scale
    ki_ref[0] = jnp.where(row == 0, nyq, ki) * scale


def _hyfilt(seq, fwd, p):
    z, tn, dl = _filter_features(seq)
    dt = 512
    nd = D_HY // dt
    hidn = HY_FILTER_HIDDEN
    const = lambda o, j: (0, 0)
    w3 = p["hy_ffn_w3"]
    b3 = p["hy_ffn_b3"][None, :]
    in_specs = [pl.BlockSpec((seq, HY_EMB_PAD), const),
                pl.BlockSpec((seq, 1), const),
                pl.BlockSpec((1, dt), lambda o, j: (0, j)),
                pl.BlockSpec((HY_EMB_PAD, hidn), const),
                pl.BlockSpec((1, hidn), const),
                pl.BlockSpec((hidn, hidn), const),
                pl.BlockSpec((1, hidn), const),
                pl.BlockSpec((1, hidn), const),
                pl.BlockSpec((hidn, dt), lambda o, j: (0, o * nd + j)),
                pl.BlockSpec((1, dt), lambda o, j: (0, o * nd + j)),
                pl.BlockSpec((hidn, dt), lambda o, j: (0, (HY_ORDER + o) * nd + j)),
                pl.BlockSpec((1, dt), lambda o, j: (0, (HY_ORDER + o) * nd + j)),
                pl.BlockSpec((2 * seq, seq), const)]
    w1p = jnp.zeros((HY_EMB_PAD, hidn), F32).at[:HY_EMB_DIM].set(p["hy_ffn_w1"])
    out_spec = pl.BlockSpec((1, seq, dt), lambda o, j: (o, 0, j))
    return pl.pallas_call(
        functools.partial(_hyfilt_body, seq=seq),
        grid=(HY_ORDER, nd),
        in_specs=in_specs,
        out_specs=[out_spec, out_spec],
        out_shape=[jax.ShapeDtypeStruct((HY_ORDER, seq, D_HY), F32)] * 2,
        compiler_params=_params(("parallel", "parallel"), 48),
        name="hyfilt",
    )(z, tn, dl, w1p, p["hy_ffn_b1"][None, :], p["hy_ffn_w2"], p["hy_ffn_b2"][None, :],
      p["hy_sin_freq"][None, :], w3, b3, w3, b3, fwd)


def _hyena_body(x1_ref, x2_ref, v_ref, w1_ref, w2_ref, wv_ref, b1_ref, b2_ref, bv_ref,
                kr_ref, ki_ref, sk_ref, f_ref, ft_ref, o_ref, *, seq):
    shape = v_ref.shape[1:]
    row = lax.broadcasted_iota(jnp.int32, shape, 0)
    first = row == 0
    last = row == seq - 1

    def conv3(x_ref, w_ref, b_ref):
        x = x_ref[0]
        xm = jnp.where(first, 0.0, pltpu.roll(x, 1, 0))
        xp = jnp.where(last, 0.0, pltpu.roll(x, seq - 1, 0))
        return w_ref[0:1, :] * xm + w_ref[1:2, :] * x + w_ref[2:3, :] * xp + b_ref[...]

    def long_conv(u, o):
        kr = kr_ref[o]
        ki = ki_ref[o]
        kiz = jnp.where(first, 0.0, ki)
        krb = jnp.where(first, ki, kr)
        uf = _dot(f_ref[...], u.astype(BF16))
        top = uf[:seq]
        bot = uf[seq:]
        y = jnp.concatenate([top * kr - bot * kiz, top * kiz + bot * krb], axis=0).astype(BF16)
        return _dot(ft_ref[...], y) + u * sk_ref[o:o + 1, :]

    z = conv3(x1_ref, w1_ref, b1_ref) * long_conv(conv3(v_ref, wv_ref, bv_ref), 0)
    z = conv3(x2_ref, w2_ref, b2_ref) * long_conv(z, 1)
    o_ref[0] = z.astype(BF16)


def _hyena(hy, cw, cb, kr, ki, skip, fwd, inv, dt):
    b, seq, _ = hy.shape
    nd = D_HY // dt
    const = lambda bi, j: (0, 0)
    part = lambda k: (lambda bi, j: (bi, 0, k * nd + j))
    wpart = lambda k: (lambda bi, j: (0, k * nd + j))
    in_specs = ([pl.BlockSpec((1, seq, dt), part(k)) for k in range(3)]
                + [pl.BlockSpec((3, dt), wpart(k)) for k in range(3)]
                + [pl.BlockSpec((1, dt), wpart(k)) for k in range(3)]
                + [pl.BlockSpec((HY_ORDER, seq, dt), lambda bi, j: (0, 0, j)),
                   pl.BlockSpec((HY_ORDER, seq, dt), lambda bi, j: (0, 0, j)),
                   pl.BlockSpec((HY_ORDER, dt), lambda bi, j: (0, j)),
                   pl.BlockSpec((2 * seq, seq), const, pipeline_mode=pl.Buffered(1)),
                   pl.BlockSpec((seq, 2 * seq), const, pipeline_mode=pl.Buffered(1))])
    return pl.pallas_call(
        functools.partial(_hyena_body, seq=seq),
        grid=(b, nd),
        in_specs=in_specs,
        out_specs=pl.BlockSpec((1, seq, dt), lambda bi, j: (bi, 0, j)),
        out_shape=jax.ShapeDtypeStruct((b, seq, D_HY), BF16),
        compiler_params=_params(("parallel", "parallel"), 56),
        name="hyena",
    )(hy, hy, hy, cw, cw, cw, cb, cb, cb, kr, ki, skip, fwd, inv)


def _layer_norm(x, g, b):
    mu = jnp.mean(x, axis=-1, keepdims=True)
    xc = x - mu
    var = jnp.mean(xc * xc, axis=-1, keepdims=True)
    return xc * lax.rsqrt(var + LN_EPS) * g + b


def _merge_body(*refs, has_pos):
    if has_pos:
        (x_ref, pos_ref, mod_ref, yr_ref, gg_ref, z_ref, ga_ref, gb_ref, wa_ref, wb_ref, wo_ref,
         lg_ref, lb_ref, wq_ref, x1_ref, h2_ref, q_ref) = refs
    else:
        (x_ref, mod_ref, yr_ref, gg_ref, z_ref, ga_ref, gb_ref, wa_ref, wb_ref, wo_ref,
         lg_ref, lb_ref, wq_ref, x1_ref, h2_ref, q_ref) = refs
    d = D_MODEL
    mod = mod_ref[0]
    g1 = mod[:, 2 * d:3 * d]
    sh2 = mod[:, 3 * d:4 * d]
    sc2 = mod[:, 4 * d:5 * d]
    ya = _dot((yr_ref[...].astype(BF16) * gg_ref[0]), wa_ref[...])
    yb = _dot(z_ref[0], wb_ref[...])
    m = ga_ref[0].astype(F32) * ya + gb_ref[0].astype(F32) * yb
    y = _dot(m.astype(BF16), wo_ref[...])
    x = x_ref[0]
    if has_pos:
        x = x + pos_ref[...]
    x1 = _layer_norm(DN_ALPHA * x + g1 * y, lg_ref[...], lb_ref[...])
    x1_ref[0] = x1
    h2 = (x1 * (1.0 + sc2) + sh2).astype(BF16)
    h2_ref[0] = h2
    q_ref[0] = _dot(h2, wq_ref[...]).astype(BF16)


def _merge(x, pos, mod3, mod_row_of_batch, yr2, gg, z, ga, gb, wa, wb, wo, lg, lb, wq, tl):
    b, l, d = x.shape
    nq = wq.shape[1]
    has_pos = pos is not None
    tok = lambda bi, ti: (bi, ti, 0)
    const = lambda bi, ti: (0, 0)
    in_specs = [pl.BlockSpec((1, tl, d), tok)]
    args = [x]
    if has_pos:
        in_specs.append(pl.BlockSpec((tl, d), lambda bi, ti: (ti, 0)))
        args.append(pos)
    in_specs += [pl.BlockSpec((1, 1, 6 * d), lambda bi, ti: (mod_row_of_batch(bi), 0, 0)),
                 pl.BlockSpec((tl, D_RNN), lambda bi, ti: (ti, bi)),
                 pl.BlockSpec((1, tl, d), tok), pl.BlockSpec((1, tl, d), tok),
                 pl.BlockSpec((1, tl, d), tok), pl.BlockSpec((1, tl, d), tok),
                 pl.BlockSpec((d, d), const), pl.BlockSpec((d, d), const), pl.BlockSpec((d, d), const),
                 pl.BlockSpec((1, d), const), pl.BlockSpec((1, d), const),
                 pl.BlockSpec((d, nq), const)]
    args += [mod3, yr2, gg, z, ga, gb, wa, wb, wo, lg, lb, wq]
    return pl.pallas_call(
        functools.partial(_merge_body, has_pos=has_pos),
        grid=(b, l // tl),
        in_specs=in_specs,
        out_specs=[pl.BlockSpec((1, tl, d), tok), pl.BlockSpec((1, tl, d), tok),
                   pl.BlockSpec((1, tl, nq), tok)],
        out_shape=[jax.ShapeDtypeStruct((b, l, d), F32), jax.ShapeDtypeStruct((b, l, d), BF16),
                   jax.ShapeDtypeStruct((b, l, nq), BF16)],
        compiler_params=_params(("parallel", "parallel"), 56),
        name="merge",
    )(*args)


RANK_NONE = 1.0e9


def _topk_rank(s, k, exact):
    n = s.shape[0]
    rank = jnp.full(s.shape, RANK_NONE, F32)
    vals = []
    if exact:
        iota = lax.broadcasted_iota(jnp.int32, s.shape, 0).astype(F32)
    for r in range(k):
        m = jnp.max(s, axis=0, keepdims=True)
        hit = s == m
        if exact:
            idx = jnp.min(jnp.where(hit, iota, float(n)), axis=0, keepdims=True)
            hit = iota == idx
        rank = jnp.where(hit, float(r), rank)
        s = jnp.where(hit, NEG_INF, s)
        vals.append(m)
    if exact:
        ties = jnp.zeros((1, s.shape[1]), F32)
    else:
        taken = jnp.sum(jnp.where(rank < RANK_NONE, 1.0, 0.0), axis=0, keepdims=True)
        ties = jnp.where(taken > float(k), 1.0, 0.0)
    return vals, rank, ties


def _pack_pair(x):
    bits = pltpu.bitcast(x.astype(BF16).astype(F32), jnp.uint32)
    return bits | (bits >> 16)


def _select_tile(s1, s2, exact):
    k = PEER_TOPK
    v1, rank1, t1 = _topk_rank(s1, k, exact)
    v2, rank2, t2 = _topk_rank(s2, k, exact)
    v2_16 = jnp.concatenate(v2, axis=0)
    v2_8 = v2_16[0:8]
    v1_hi = jnp.concatenate(v1[8:16], axis=0)
    cand = jnp.concatenate([v1[0] + v2_16] + [v1[a] + v2_8 for a in range(1, 8)] + [v1_hi + v2[0]], axis=0)
    _, rankc, tc = _topk_rank(cand, k, exact)
    sel = jnp.where(rankc < float(k), 1.0, 0.0)
    z = jnp.sum(sel * jnp.exp(cand - cand[0:1]), axis=0, keepdims=True)
    counts = [jnp.sum(sel[0:16], axis=0, keepdims=True)]
    counts += [jnp.sum(sel[8 + 8 * a:16 + 8 * a], axis=0, keepdims=True) for a in range(1, 8)]
    counts += [sel[72 + a:73 + a] for a in range(8)]
    ni = jnp.zeros(s1.shape, F32)
    for a in range(k):
        ni = jnp.where(rank1 == float(a), counts[a], ni)
    e1n = jnp.exp(s1 - v1[0]) / z
    e2 = jnp.exp(s2 - v2[0])
    return _pack_pair(e1n), _pack_pair(ni), e2.astype(BF16), rank2.astype(BF16), t1 + t2 + tc


def _peersel_body(q_ref, k_ref, e1n_ref, ni_ref, e2_ref, r2_ref, s1_ref, s2_ref):
    nt = (((1,), (1,)), ((), ()))
    s1_ref[...] = lax.dot_general(k_ref[0], q_ref[:, 0:PEER_HALF], nt, preferred_element_type=F32)
    s2_ref[...] = lax.dot_general(k_ref[1], q_ref[:, PEER_HALF:2 * PEER_HALF], nt, preferred_element_type=F32)
    lane = 128
    n_tiles = s1_ref.shape[1] // lane

    def run(exact):
        ties = jnp.zeros((1, lane), F32)
        for lt in range(n_tiles):
            cols = slice(lt * lane, (lt + 1) * lane)
            e1n, ni, e2, r2, t = _select_tile(s1_ref[:, cols], s2_ref[:, cols], exact)
            e1n_ref[0, :, cols] = e1n
            ni_ref[0, :, cols] = ni
            e2_ref[0, :, cols] = e2
            r2_ref[0, :, cols] = r2
            ties = ties + t
        return ties

    ties = run(exact=False)

    @pl.when(jnp.max(ties) > 0.0)
    def _():
        run(exact=True)


def _peersel(q2, keys_bf, tt):
    t = q2.shape[0]
    out_spec = pl.BlockSpec((1, N_KEYS, tt), lambda ti, h: (h, 0, ti))
    return pl.pallas_call(
        _peersel_body,
        grid=(t // tt, PEER_HEADS),
        in_specs=[pl.BlockSpec((tt, 2 * PEER_HALF), lambda ti, h: (ti, h)),
                  pl.BlockSpec((2, N_KEYS, PEER_HALF), lambda ti, h: (0, 0, 0))],
        out_specs=[out_spec] * 4,
        out_shape=[jax.ShapeDtypeStruct((PEER_HEADS, N_KEYS, t), jnp.uint32)] * 2
        + [jax.ShapeDtypeStruct((PEER_HEADS, N_KEYS, t), BF16)] * 2,
        scratch_shapes=[pltpu.VMEM((N_KEYS, tt), F32), pltpu.VMEM((N_KEYS, tt), F32)],
        compiler_params=_params(("parallel", "parallel"), 48),
        name="peersel",
    )(q2, keys_bf)


PEER_EB = 2048
PEER_EG = 256


def _peermix_body(h2_ref, u_ref, vt_ref, e1n_ref, ni_ref, e2_ref, r2_ref, x1_ref, mod_ref, lg_ref, lb_ref,
                  o_ref, acc_ref, w_ref, act_ref, g_ref):
    e = pl.program_id(1)
    tt = acc_ref.shape[1]
    n_groups = PEER_EB // PEER_EG
    slabs_per_group = PEER_EG // N_KEYS
    pack = 16
    n_chunks = N_KEYS // pack
    lanes = 256
    zero = jnp.zeros((), BF16)
    nt = (((1,), (1,)), ((), ()))

    @pl.when(e == 0)
    def _():
        acc_ref[...] = jnp.zeros(acc_ref.shape, F32)

    w_ref[0] = jnp.zeros(w_ref.shape[1:], BF16)
    act_ref[1] = jnp.zeros(act_ref.shape[1:], F32)
    g_ref[1] = jnp.zeros(g_ref.shape[1:], BF16)

    def activate(slot):
        for c in range(PEER_EG // pack):
            rows = slice(c * pack, (c + 1) * pack)
            w_ref[slot, rows, :] = _gelu(act_ref[slot, rows, :]).astype(BF16) * g_ref[slot, rows, :]

    def stage(p, cur):
        prev = 1 - cur
        acc_ref[...] += _dot(vt_ref[jnp.maximum(p - 2, 0)], w_ref[cur])
        activate(prev)
        u_p = u_ref[pl.ds(pl.multiple_of(p * PEER_EG, PEER_EG), PEER_EG), :]
        act_ref[cur] = lax.dot_general(h2_ref[...], u_p, nt, preferred_element_type=F32).T
        for s2, lh in [(a, b) for a in range(slabs_per_group) for b in range(tt // lanes)]:
            s = p * slabs_per_group + s2
            cols = slice(lh * lanes, (lh + 1) * lanes)
            g = [None] * n_chunks
            for h in range(PEER_HEADS):
                nrow = pltpu.bitcast(jnp.broadcast_to(ni_ref[h, pl.ds(s, 1), cols], (pack // 2, lanes)), BF16)
                erow = pltpu.bitcast(jnp.broadcast_to(e1n_ref[h, pl.ds(s, 1), cols], (pack // 2, lanes)), BF16)
                for c in range(n_chunks):
                    rows = slice(c * pack, (c + 1) * pack)
                    term = jnp.where(r2_ref[h, rows, cols] < nrow, e2_ref[h, rows, cols], zero) * erow
                    g[c] = term if g[c] is None else g[c] + term
            for c in range(n_chunks):
                g_ref[cur, s2 * N_KEYS + c * pack:s2 * N_KEYS + (c + 1) * pack, cols] = g[c]

    def pair(i, carry):
        stage(2 * i, 0)
        stage(2 * i + 1, 1)
        return carry

    for i in range(n_groups // 2):
        pair(i, 0)
    acc_ref[...] += _dot(vt_ref[n_groups - 2], w_ref[0])
    activate(1)
    acc_ref[...] += _dot(vt_ref[n_groups - 1], w_ref[1])

    @pl.when(e == pl.num_programs(1) - 1)
    def _():
        d = D_MODEL
        g2 = mod_ref[0][:, 5 * d:6 * d]
        y = acc_ref[...].T
        o_ref[...] = _layer_norm(DN_ALPHA * x1_ref[...] + g2 * y, lg_ref[...], lb_ref[...])


def _peermix(h2, u_bf, vt_bf, e1n, ni, e2, r2, x1, mod3, mod_row_of_tile, lg, lb, tt):
    t, d = h2.shape
    ne = u_bf.shape[0]
    ns = PEER_EB // N_KEYS
    tile = lambda ti, e: (ti, 0)
    const = lambda ti, e: (0, 0)
    return pl.pallas_call(
        _peermix_body,
        grid=(t // tt, ne // PEER_EB),
        in_specs=[pl.BlockSpec((tt, d), tile),
                  pl.BlockSpec((PEER_EB, d), lambda ti, e: (e, 0)),
                  pl.BlockSpec((PEER_EB // PEER_EG, d, PEER_EG), lambda ti, e: (e, 0, 0)),
                  pl.BlockSpec((PEER_HEADS, ns, tt), lambda ti, e: (0, e, ti)),
                  pl.BlockSpec((PEER_HEADS, ns, tt), lambda ti, e: (0, e, ti)),
                  pl.BlockSpec((PEER_HEADS, N_KEYS, tt), lambda ti, e: (0, 0, ti)),
                  pl.BlockSpec((PEER_HEADS, N_KEYS, tt), lambda ti, e: (0, 0, ti)),
                  pl.BlockSpec((tt, d), tile),
                  pl.BlockSpec((1, 1, 6 * d), lambda ti, e: (mod_row_of_tile(ti), 0, 0)),
                  pl.BlockSpec((1, d), const), pl.BlockSpec((1, d), const)],
        out_specs=pl.BlockSpec((tt, d), tile),
        out_shape=jax.ShapeDtypeStruct((t, d), F32),
        scratch_shapes=[pltpu.VMEM((d, tt), F32), pltpu.VMEM((2, PEER_EG, tt), BF16),
                        pltpu.VMEM((2, PEER_EG, tt), F32), pltpu.VMEM((2, PEER_EG, tt), BF16)],
        compiler_params=_params(("parallel", "arbitrary"), 56),
        name="peermix",
    )(h2, u_bf, vt_bf, e1n, ni, e2, r2, x1, mod3, lg, lb)


def _prep_params(p):
    hd = RNN_HEAD_DIM
    q = dict(p)
    q["w_in_bf"] = p["w_in"].astype(BF16)
    q["wg"] = jnp.transpose(p["rnn_gate_w"], (2, 3, 0, 1, 4)).reshape(N_RNN_HEADS, hd, 4 * hd).astype(BF16)
    q["bg"] = jnp.transpose(p["rnn_gate_b"].reshape(2, 2, N_RNN_HEADS, hd), (2, 0, 1, 3)).reshape(
        N_RNN_HEADS, 1, 4 * hd)
    q["lam"] = jnp.transpose(p["rnn_lambda"].reshape(2, N_RNN_HEADS, hd), (1, 0, 2))
    q["rnn_cb"] = p["rnn_conv_b"][None, :]
    q["hy_cb"] = p["hy_conv_b"][None, :]
    q["wa"] = p["w_branch_a"].astype(BF16)
    q["wb"] = p["w_branch_b"].astype(BF16)
    q["wo"] = p["w_out"].astype(BF16)
    q["wq"] = p["peer_w_query"].astype(BF16)
    q["ln1_g2"] = p["ln1_g"][None, :]
    q["ln1_b2"] = p["ln1_b"][None, :]
    q["ln2_g2"] = p["ln2_g"][None, :]
    q["ln2_b2"] = p["ln2_b"][None, :]
    return q


def _mixer_group(x, pos, mod3, mod_row_of_batch, h0, q, tl, bb, dt):
    b, l, _ = x.shape
    fwd, inv = _dft_matrices(l)
    kr, ki = _hyfilt(l, fwd, q)
    rx2, gg, hy, ga, gb = _inproj(x, pos, mod3, q["w_in_bf"], mod_row_of_batch, tl)
    y3, st = _rglru(rx2.reshape(l, b, D_RNN), h0, q["wg"], q["bg"], q["lam"], q["rnn_conv_w"],
                    q["rnn_cb"], bb)
    z = _hyena(hy, q["hy_conv_w"], q["hy_cb"], kr, ki, q["hy_skip"], fwd, inv, dt)
    x1, h2, qq = _merge(x, pos, mod3, mod_row_of_batch, y3.reshape(l, b * D_RNN), gg, z, ga, gb,
                        q["wa"], q["wb"], q["wo"], q["ln1_g2"], q["ln1_b2"], q["wq"], tl)
    return x1, h2, qq, st


def _group_transposed(v):
    ne, d = v.shape
    return jnp.transpose(v.astype(BF16).reshape(ne // PEER_EG, PEER_EG, d), (0, 2, 1))


def _peer_group(x1, h2, qq, mod3, mod_row_of_tile, q, tt_sel, tt_mix):
    b, l, d = x1.shape
    t = b * l
    e1n, ni, e2, r2 = _peersel(qq.reshape(t, qq.shape[-1]), q["keys_bf"], tt_sel)
    out = _peermix(h2.reshape(t, d), q["u_bf"], q["vt_bf"], e1n, ni, e2, r2, x1.reshape(t, d), mod3,
                   mod_row_of_tile, q["ln2_g2"], q["ln2_b2"], tt_mix)
    return out.reshape(b, l, d)


def _grid_pos_embed(n_tokens):
    rows = n_tokens // GRID_W
    t = np.arange(rows * GRID_W)
    r = (t // GRID_W).astype(np.float32)
    col = (t % GRID_W).astype(np.float32)
    quarter = D_MODEL // 4
    omega = (1.0 / (10000.0 ** (np.arange(quarter, dtype=np.float32) / np.float32(quarter)))).astype(np.float32)
    er = r[:, None] * omega[None, :]
    ec = col[:, None] * omega[None, :]
    return jnp.asarray(np.concatenate([np.sin(er), np.cos(er), np.sin(ec), np.cos(ec)], axis=-1), dtype=F32)


PARAM_NAMES = ("w_ada", "b_ada", "w_in", "rnn_conv_w", "rnn_conv_b", "rnn_gate_w", "rnn_gate_b", "rnn_lambda",
               "hy_conv_w", "hy_conv_b", "hy_ffn_w1", "hy_ffn_b1", "hy_ffn_w2", "hy_ffn_b2", "hy_ffn_w3",
               "hy_ffn_b3", "hy_sin_freq", "hy_skip", "w_branch_a", "w_branch_b", "w_out", "ln1_g", "ln1_b",
               "ln2_g", "ln2_b", "peer_w_query", "peer_sub_keys", "peer_u", "peer_v")


def kernel(x_prompt, x_sample, state_rglru, c, c_ctx, w_ada, b_ada, w_in, rnn_conv_w, rnn_conv_b, rnn_gate_w,
           rnn_gate_b, rnn_lambda, hy_conv_w, hy_conv_b, hy_ffn_w1, hy_ffn_b1, hy_ffn_w2, hy_ffn_b2, hy_ffn_w3,
           hy_ffn_b3, hy_sin_freq, hy_skip, w_branch_a, w_branch_b, w_out, ln1_g, ln1_b, ln2_g, ln2_b,
           peer_w_query, peer_sub_keys, peer_u, peer_v):
    stacked = dict(zip(PARAM_NAMES, (w_ada, b_ada, w_in, rnn_conv_w, rnn_conv_b, rnn_gate_w, rnn_gate_b,
                                     rnn_lambda, hy_conv_w, hy_conv_b, hy_ffn_w1, hy_ffn_b1, hy_ffn_w2,
                                     hy_ffn_b2, hy_ffn_w3, hy_ffn_b3, hy_sin_freq, hy_skip, w_branch_a,
                                     w_branch_b, w_out, ln1_g, ln1_b, ln2_g, ln2_b, peer_w_query,
                                     peer_sub_keys, peer_u, peer_v)))
    depth = w_ada.shape[0]
    bp, lp, d = x_prompt.shape
    bs, ls, _ = x_sample.shape
    assert bs + 1 <= MOD_ROWS
    cond = jnp.zeros((MOD_ROWS, d), F32).at[0].set(c_ctx).at[1:1 + bs].set(c)
    pos = _grid_pos_embed(ls)
    tt_mix = 512
    sample_tiles_per_batch = ls // tt_mix

    xp, xs = x_prompt, x_sample
    ctx_states = []
    for layer in range(depth):
        q = _prep_params({name: w[layer] for name, w in stacked.items()})
        q["keys_bf"] = q["peer_sub_keys"].astype(BF16)
        q["u_bf"] = q["peer_u"].astype(BF16)
        q["vt_bf"] = _group_transposed(q["peer_v"])
        mod3 = _ada(cond, q["w_ada"], q["b_ada"][None, :]).reshape(MOD_ROWS, 1, 6 * d)

        x1, h2, qq, st = _mixer_group(xp, None, mod3, lambda bi: 0, jnp.zeros((2, bp, D_RNN), F32), q,
                                      tl=lp, bb=32, dt=1024)
        xp = _peer_group(x1, h2, qq, mod3, lambda ti: 0, q, 256, tt_mix)
        ctx_states.append(jnp.transpose(st, (1, 0, 2)))

        pos_l = pos if layer == 0 else None
        h0 = jnp.transpose(state_rglru[:, layer], (1, 0, 2))
        x1, h2, qq, _ = _mixer_group(xs, pos_l, mod3, lambda bi: bi + 1, h0, q, tl=512, bb=8, dt=256)
        xs = _peer_group(x1, h2, qq, mod3, lambda ti: 1 + ti // sample_tiles_per_batch, q, 256, tt_mix)

    new_state = jnp.stack(ctx_states, axis=1).astype(x_prompt.dtype)
    return (xp, xs, new_state)
```

```python
import functools
import math

import numpy as np
import jax
import jax.numpy as jnp
from jax import lax
from jax.experimental import pallas as pl
from jax.experimental.pallas import tpu as pltpu

F32 = jnp.float32
BF16 = jnp.bfloat16

D_MODEL = 1024
D_RNN = 1024
N_RNN_HEADS = 4
RNN_HEAD_DIM = D_RNN // N_RNN_HEADS
RNN_CONV_W = 4
RGLRU_C = 8.0
D_HY = 1024
HY_ORDER = 2
HY_EMB_BANDS = 16
HY_EMB_DIM = 1 + 2 * HY_EMB_BANDS
HY_EMB_PAD = 64
HY_FILTER_HIDDEN = 64
HY_DECAY_TARGET = 1e-2
HY_MIN_DECAY = math.log(HY_DECAY_TARGET) / 1.5
HY_MAX_DECAY = math.log(HY_DECAY_TARGET) / 0.3
GRID_W = 64
N_KEYS = 128
N_EXPERTS = N_KEYS * N_KEYS
PEER_HEADS = 8
PEER_HALF = 128
PEER_TOPK = 16
DEPTH = 1
DN_ALPHA = (2.0 * DEPTH) ** 0.25
LN_EPS = 1e-5

MOD_ROWS = 16
MIB = 1024 * 1024
NEG_INF = float("-inf")


def _params(semantics, vmem_mib):
    return pltpu.CompilerParams(dimension_semantics=semantics, vmem_limit_bytes=vmem_mib * MIB)


def _gelu(x):
    return jax.nn.gelu(x, approximate=True)


def _dot(a, b):
    return jnp.dot(a, b, preferred_element_type=F32)


def _dot_f32(a, b):
    return jnp.dot(a, b, preferred_element_type=F32, precision=lax.Precision.HIGHEST)


def _ada_body(c_ref, w_ref, b_ref, o_ref):
    c = c_ref[...]
    o_ref[...] = _dot_f32(c * jax.nn.sigmoid(c), w_ref[...]) + b_ref[...]


def _ada(cond, w_ada, b_ada):
    n = w_ada.shape[1]
    tn = 1024
    return pl.pallas_call(
        _ada_body,
        grid=(n // tn,),
        in_specs=[pl.BlockSpec((MOD_ROWS, D_MODEL), lambda j: (0, 0)),
                  pl.BlockSpec((D_MODEL, tn), lambda j: (0, j)),
                  pl.BlockSpec((1, tn), lambda j: (0, j))],
        out_specs=pl.BlockSpec((MOD_ROWS, tn), lambda j: (0, j)),
        out_shape=jax.ShapeDtypeStruct((MOD_ROWS, n), F32),
        compiler_params=_params(("arbitrary",), 32),
        name="ada",
    )(cond, w_ada, b_ada)


def _inproj_body(*refs, has_pos):
    if has_pos:
        x_ref, pos_ref, mod_ref, w_ref, rx_ref, gg_ref, hy_ref, ga_ref, gb_ref = refs
    else:
        x_ref, mod_ref, w_ref, rx_ref, gg_ref, hy_ref, ga_ref, gb_ref = refs
    x = x_ref[0]
    if has_pos:
        x = x + pos_ref[...]
    mod = mod_ref[0]
    sh1 = mod[:, 0:D_MODEL]
    sc1 = mod[:, D_MODEL:2 * D_MODEL]
    h = (x * (1.0 + sc1) + sh1).astype(BF16)
    o = 0
    rx_ref[...] = _dot(h, w_ref[:, o:o + D_RNN])
    o += D_RNN
    gg_ref[0] = _gelu(_dot(h, w_ref[:, o:o + D_RNN])).astype(BF16)
    o += D_RNN
    for j in range(3):
        hy_ref[0, :, j * D_HY:(j + 1) * D_HY] = _dot(h, w_ref[:, o:o + D_HY])
        o += D_HY
    ga_ref[0] = jax.nn.sigmoid(_dot(h, w_ref[:, o:o + D_MODEL])).astype(BF16)
    o += D_MODEL
    gb_ref[0] = jax.nn.sigmoid(_dot(h, w_ref[:, o:o + D_MODEL])).astype(BF16)


def _inproj(x, pos, mod3, w_in_bf, mod_row_of_batch, tl):
    b, l, _ = x.shape
    d_in = w_in_bf.shape[1]
    has_pos = pos is not None
    in_specs = [pl.BlockSpec((1, tl, D_MODEL), lambda bi, ti: (bi, ti, 0))]
    args = [x]
    if has_pos:
        in_specs.append(pl.BlockSpec((tl, D_MODEL), lambda bi, ti: (ti, 0)))
        args.append(pos)
    in_specs += [pl.BlockSpec((1, 1, 6 * D_MODEL), lambda bi, ti: (mod_row_of_batch(bi), 0, 0)),
                 pl.BlockSpec((D_MODEL, d_in), lambda bi, ti: (0, 0), pipeline_mode=pl.Buffered(1))]
    args += [mod3, w_in_bf]
    tok = lambda bi, ti: (bi, ti, 0)
    out_specs = [pl.BlockSpec((tl, D_RNN), lambda bi, ti: (ti, bi)),
                 pl.BlockSpec((1, tl, D_RNN), tok),
                 pl.BlockSpec((1, tl, 3 * D_HY), tok),
                 pl.BlockSpec((1, tl, D_MODEL), tok),
                 pl.BlockSpec((1, tl, D_MODEL), tok)]
    out_shape = [jax.ShapeDtypeStruct((l, b * D_RNN), F32),
                 jax.ShapeDtypeStruct((b, l, D_RNN), BF16),
                 jax.ShapeDtypeStruct((b, l, 3 * D_HY), F32),
                 jax.ShapeDtypeStruct((b, l, D_MODEL), BF16),
                 jax.ShapeDtypeStruct((b, l, D_MODEL), BF16)]
    return pl.pallas_call(
        functools.partial(_inproj_body, has_pos=has_pos),
        grid=(b, l // tl),
        in_specs=in_specs, out_specs=out_specs, out_shape=out_shape,
        compiler_params=_params(("parallel", "parallel"), 56),
        name="inproj",
    )(*args)


def _softplus(x):
    return jnp.maximum(x, 0.0) + jnp.log(1.0 + jnp.exp(-jnp.abs(x)))


def _rglru_body(rx_ref, h0_ref, wg_ref, bg_ref, lam_ref, cw_ref, cb_ref, y_ref, st_ref, a_s, u_s,
                *, seq, bb, tc):
    hd = RNN_HEAD_DIM
    nchunks = seq // tc
    cw = cw_ref[...]
    cb = cb_ref[...]
    for d in range(2):
        coef = -RGLRU_C * _softplus(-lam_ref[0, d:d + 1, :])
        wd = wg_ref[0, :, 2 * d * hd:2 * (d + 1) * hd]
        bd = bg_ref[0, :, 2 * d * hd:2 * (d + 1) * hd]

        def chunk(ci, h, d=d, coef=coef, wd=wd, bd=bd):
            c = ci if d == 0 else nchunks - 1 - ci
            t0 = pl.multiple_of(c * tc, tc)
            lo_ok = jnp.where(c > 0, 1.0, 0.0)
            hi_ok = jnp.where(c < nchunks - 1, 1.0, 0.0)
            lo = rx_ref[pl.ds(jnp.maximum(t0 - 2, 0), 2)] * lo_ok
            hi = rx_ref[pl.ds(jnp.minimum(t0 + tc, seq - 1), 1)] * hi_ok
            xe = jnp.concatenate([lo, rx_ref[pl.ds(t0, tc)], hi], axis=0)
            xc = cb[None] + sum(cw[k:k + 1][None] * xe[k:k + tc] for k in range(RNN_CONV_W))
            xc2 = xc.reshape(tc * bb, hd)
            g = _dot(xc2.astype(BF16), wd) + bd
            r = jax.nn.sigmoid(g[:, :hd])
            i = jax.nn.sigmoid(g[:, hd:])
            a = jnp.exp(coef * r)
            u = jnp.sqrt(1.0 - a * a) * (i * xc2)
            a_s[...] = a.reshape(tc, bb, hd)
            u_s[...] = u.reshape(tc, bb, hd)

            def step(j, h):
                tau = j if d == 0 else tc - 1 - j
                h = a_s[tau] * h + u_s[tau]
                if d == 0:
                    y_ref[t0 + tau] = h
                else:
                    y_ref[t0 + tau] = y_ref[t0 + tau] + h
                return h

            return lax.fori_loop(0, tc, step, h, unroll=8)

        h_fin = lax.fori_loop(0, nchunks, chunk, h0_ref[d])
        st_ref[d] = h_fin


def _rglru(rx3, h0, wg, bg, lam, cw, cb, bb):
    seq, b, _ = rx3.shape
    hd = RNN_HEAD_DIM
    tc = max(8, 512 // bb)
    blk = lambda bi, hi: (0, bi, hi)
    return pl.pallas_call(
        functools.partial(_rglru_body, seq=seq, bb=bb, tc=tc),
        grid=(b // bb, N_RNN_HEADS),
        in_specs=[pl.BlockSpec((seq, bb, hd), blk),
                  pl.BlockSpec((2, bb, hd), blk),
                  pl.BlockSpec((1, hd, 4 * hd), lambda bi, hi: (hi, 0, 0)),
                  pl.BlockSpec((1, 1, 4 * hd), lambda bi, hi: (hi, 0, 0)),
                  pl.BlockSpec((1, 2, hd), lambda bi, hi: (hi, 0, 0)),
                  pl.BlockSpec((RNN_CONV_W, hd), lambda bi, hi: (0, hi)),
                  pl.BlockSpec((1, hd), lambda bi, hi: (0, hi))],
        out_specs=[pl.BlockSpec((seq, bb, hd), blk),
                   pl.BlockSpec((2, bb, hd), blk)],
        out_shape=[jax.ShapeDtypeStruct((seq, b, D_RNN), F32),
                   jax.ShapeDtypeStruct((2, b, D_RNN), F32)],
        scratch_shapes=[pltpu.VMEM((tc, bb, hd), F32), pltpu.VMEM((tc, bb, hd), F32)],
        compiler_params=_params(("parallel", "parallel"), 56),
        name="rglru",
    )(rx3, h0, wg, bg, lam, cw, cb)


def _dft_matrices(seq):
    k = np.arange(seq, dtype=np.float64)[:, None]
    s = np.arange(seq, dtype=np.float64)[None, :]
    ang = np.pi * k * s / seq
    top = np.cos(ang)
    bot = -np.sin(ang)
    bot[0, :] = np.where(np.arange(seq) % 2 == 0, 1.0, -1.0)
    fwd = np.concatenate([top, bot], axis=0)
    return jnp.asarray(fwd, dtype=BF16), jnp.asarray(fwd.T, dtype=BF16)


def _filter_features(seq):
    t = np.arange(seq, dtype=np.float32)
    t_norm = t / np.float32(max(seq - 1, 1))
    w = (np.float32(2.0 * math.pi) * t / np.float32(seq)).astype(np.float32)
    bands = np.linspace(1e-4, HY_EMB_BANDS - 1, HY_EMB_BANDS, dtype=np.float32)
    fw = w[:, None] * bands[None, :]
    z = np.concatenate([t_norm[:, None], np.cos(fw), -np.sin(fw)], axis=-1).astype(np.float32)
    zp = np.zeros((seq, HY_EMB_PAD), np.float32)
    zp[:, :HY_EMB_DIM] = z
    deltas = np.abs(np.linspace(HY_MIN_DECAY, HY_MAX_DECAY, D_HY, dtype=np.float32))
    return jnp.asarray(zp), jnp.asarray(t_norm[:, None]), jnp.asarray(deltas[None, :])


def _dot_split(f_bf, x):
    hi = x.astype(BF16)
    lo = (x - hi.astype(F32)).astype(BF16)
    return _dot(f_bf, hi) + _dot(f_bf, lo)


def _hyfilt_body(z_ref, tn_ref, dl_ref, w1_ref, b1_ref, w2_ref, b2_ref, fr_ref, w3f_ref, b3f_ref,
                 w3b_ref, b3b_ref, f_ref, kr_ref, ki_ref, *, seq):
    freq = fr_ref[...]
    hid = jnp.sin(freq * (_dot_f32(z_ref[...], w1_ref[...]) + b1_ref[...]))
    hid = jnp.sin(freq * (_dot_f32(hid, w2_ref[...]) + b2_ref[...]))
    decay = jnp.exp(-tn_ref[...] * dl_ref[...])
    ff = (_dot_f32(hid, w3f_ref[...]) + b3f_ref[...]) * decay
    fb = (_dot_f32(hid, w3b_ref[...]) + b3b_ref[...]) * decay
    row = lax.broadcasted_iota(jnp.int32, ff.shape, 0)
    fb = jnp.where(row == 0, 0.0, fb)
    s = ff + fb
    dm = ff - fb
    sign = jnp.where((row & 1) == 0, 1.0, -1.0)
    nyq = jnp.sum(s * sign, axis=0, keepdims=True)
    kr = _dot_split(f_ref[0:seq, :], s)
    ki = _dot_split(f_ref[seq:2 * seq, :], dm)
    scale = jnp.where(row == 0, 0.5 / seq, 1.0 / seq)
    kr_ref[0] = kr * scale
    ki_ref[0] = jnp.where(row == 0, nyq, ki) * scale


def _hyfilt(seq, fwd, p):
    z, tn, dl = _filter_features(seq)
    dt = 512
    nd = D_HY // dt
    hidn = HY_FILTER_HIDDEN
    const = lambda o, j: (0, 0)
    w3 = p["hy_ffn_w3"]
    b3 = p["hy_ffn_b3"][None, :]
    in_specs = [pl.BlockSpec((seq, HY_EMB_PAD), const),
                pl.BlockSpec((seq, 1), const),
                pl.BlockSpec((1, dt), lambda o, j: (0, j)),
                pl.BlockSpec((HY_EMB_PAD, hidn), const),
                pl.BlockSpec((1, hidn), const),
                pl.BlockSpec((hidn, hidn), const),
                pl.BlockSpec((1, hidn), const),
                pl.BlockSpec((1, hidn), const),
                pl.BlockSpec((hidn, dt), lambda o, j: (0, o * nd + j)),
                pl.BlockSpec((1, dt), lambda o, j: (0, o * nd + j)),
                pl.BlockSpec((hidn, dt), lambda o, j: (0, (HY_ORDER + o) * nd + j)),
                pl.BlockSpec((1, dt), lambda o, j: (0, (HY_ORDER + o) * nd + j)),
                pl.BlockSpec((2 * seq, seq), const)]
    w1p = jnp.zeros((HY_EMB_PAD, hidn), F32).at[:HY_EMB_DIM].set(p["hy_ffn_w1"])
    out_spec = pl.BlockSpec((1, seq, dt), lambda o, j: (o, 0, j))
    return pl.pallas_call(
        functools.partial(_hyfilt_body, seq=seq),
        grid=(HY_ORDER, nd),
        in_specs=in_specs,
        out_specs=[out_spec, out_spec],
        out_shape=[jax.ShapeDtypeStruct((HY_ORDER, seq, D_HY), F32)] * 2,
        compiler_params=_params(("parallel", "parallel"), 48),
        name="hyfilt",
    )(z, tn, dl, w1p, p["hy_ffn_b1"][None, :], p["hy_ffn_w2"], p["hy_ffn_b2"][None, :],
      p["hy_sin_freq"][None, :], w3, b3, w3, b3, fwd)


HY_LANES = 256


def _hyena_body(x1_ref, x2_ref, v_ref, w1_ref, w2_ref, wv_ref, b1_ref, b2_ref, bv_ref,
                kr_ref, ki_ref, sk_ref, f_ref, ft_ref, o_ref, *, seq):
    dt = v_ref.shape[2]
    row = lax.broadcasted_iota(jnp.int32, (seq, HY_LANES), 0)
    first = row == 0
    last = row == seq - 1

    def chain(cols):
        def conv3(x_ref, w_ref, b_ref):
            x = x_ref[0, :, cols]
            xm = jnp.where(first, 0.0, pltpu.roll(x, 1, 0))
            xp = jnp.where(last, 0.0, pltpu.roll(x, seq - 1, 0))
            return w_ref[0:1, cols] * xm + w_ref[1:2, cols] * x + w_ref[2:3, cols] * xp + b_ref[:, cols]

        def long_conv(u, o):
            kr = kr_ref[o, :, cols]
            ki = ki_ref[o, :, cols]
            kiz = jnp.where(first, 0.0, ki)
            krb = jnp.where(first, ki, kr)
            uf = _dot(f_ref[...], u.astype(BF16))
            top = uf[:seq]
            bot = uf[seq:]
            y = jnp.concatenate([top * kr - bot * kiz, top * kiz + bot * krb], axis=0).astype(BF16)
            return _dot(ft_ref[...], y) + u * sk_ref[o:o + 1, cols]

        z = conv3(x1_ref, w1_ref, b1_ref) * long_conv(conv3(v_ref, wv_ref, bv_ref), 0)
        z = conv3(x2_ref, w2_ref, b2_ref) * long_conv(z, 1)
        o_ref[0, :, cols] = z.astype(BF16)

    for c in range(dt // HY_LANES):
        chain(slice(c * HY_LANES, (c + 1) * HY_LANES))


def _hyena(hy, cw, cb, kr, ki, skip, fwd, inv, dt):
    b, seq, _ = hy.shape
    nd = D_HY // dt
    const = lambda j, bi: (0, 0)
    part = lambda k: (lambda j, bi: (bi, 0, k * nd + j))
    wpart = lambda k: (lambda j, bi: (0, k * nd + j))
    once = pl.Buffered(1)
    in_specs = ([pl.BlockSpec((1, seq, dt), part(k)) for k in range(3)]
                + [pl.BlockSpec((3, dt), wpart(k)) for k in range(3)]
                + [pl.BlockSpec((1, dt), wpart(k)) for k in range(3)]
                + [pl.BlockSpec((HY_ORDER, seq, dt), lambda j, bi: (0, 0, j), pipeline_mode=once),
                   pl.BlockSpec((HY_ORDER, seq, dt), lambda j, bi: (0, 0, j), pipeline_mode=once),
                   pl.BlockSpec((HY_ORDER, dt), lambda j, bi: (0, j)),
                   pl.BlockSpec((2 * seq, seq), const, pipeline_mode=once),
                   pl.BlockSpec((seq, 2 * seq), const, pipeline_mode=once)])
    return pl.pallas_call(
        functools.partial(_hyena_body, seq=seq),
        grid=(nd, b),
        in_specs=in_specs,
        out_specs=pl.BlockSpec((1, seq, dt), lambda j, bi: (bi, 0, j)),
        out_shape=jax.ShapeDtypeStruct((b, seq, D_HY), BF16),
        compiler_params=_params(("parallel", "parallel"), 56),
        name="hyena",
    )(hy, hy, hy, cw, cw, cw, cb, cb, cb, kr, ki, skip, fwd, inv)


def _layer_norm(x, g, b):
    mu = jnp.mean(x, axis=-1, keepdims=True)
    xc = x - mu
    var = jnp.mean(xc * xc, axis=-1, keepdims=True)
    return xc * lax.rsqrt(var + LN_EPS) * g + b


def _merge_body(*refs, has_pos):
    if has_pos:
        (x_ref, pos_ref, mod_ref, yr_ref, gg_ref, z_ref, ga_ref, gb_ref, wa_ref, wb_ref, wo_ref,
         lg_ref, lb_ref, wq_ref, x1_ref, h2_ref, q_ref) = refs
    else:
        (x_ref, mod_ref, yr_ref, gg_ref, z_ref, ga_ref, gb_ref, wa_ref, wb_ref, wo_ref,
         lg_ref, lb_ref, wq_ref, x1_ref, h2_ref, q_ref) = refs
    d = D_MODEL
    mod = mod_ref[0]
    g1 = mod[:, 2 * d:3 * d]
    sh2 = mod[:, 3 * d:4 * d]
    sc2 = mod[:, 4 * d:5 * d]
    ya = _dot((yr_ref[...].astype(BF16) * gg_ref[0]), wa_ref[...])
    yb = _dot(z_ref[0], wb_ref[...])
    m = ga_ref[0].astype(F32) * ya + gb_ref[0].astype(F32) * yb
    y = _dot(m.astype(BF16), wo_ref[...])
    x = x_ref[0]
    if has_pos:
        x = x + pos_ref[...]
    x1 = _layer_norm(DN_ALPHA * x + g1 * y, lg_ref[...], lb_ref[...])
    x1_ref[0] = x1
    h2 = (x1 * (1.0 + sc2) + sh2).astype(BF16)
    h2_ref[0] = h2
    q_ref[0] = _dot(h2, wq_ref[...]).astype(BF16)


def _merge(x, pos, mod3, mod_row_of_batch, yr2, gg, z, ga, gb, wa, wb, wo, lg, lb, wq, tl):
    b, l, d = x.shape
    nq = wq.shape[1]
    has_pos = pos is not None
    tok = lambda bi, ti: (bi, ti, 0)
    const = lambda bi, ti: (0, 0)
    in_specs = [pl.BlockSpec((1, tl, d), tok)]
    args = [x]
    if has_pos:
        in_specs.append(pl.BlockSpec((tl, d), lambda bi, ti: (ti, 0)))
        args.append(pos)
    in_specs += [pl.BlockSpec((1, 1, 6 * d), lambda bi, ti: (mod_row_of_batch(bi), 0, 0)),
                 pl.BlockSpec((tl, D_RNN), lambda bi, ti: (ti, bi)),
                 pl.BlockSpec((1, tl, d), tok), pl.BlockSpec((1, tl, d), tok),
                 pl.BlockSpec((1, tl, d), tok), pl.BlockSpec((1, tl, d), tok),
                 pl.BlockSpec((d, d), const), pl.BlockSpec((d, d), const), pl.BlockSpec((d, d), const),
                 pl.BlockSpec((1, d), const), pl.BlockSpec((1, d), const),
                 pl.BlockSpec((d, nq), const)]
    args += [mod3, yr2, gg, z, ga, gb, wa, wb, wo, lg, lb, wq]
    return pl.pallas_call(
        functools.partial(_merge_body, has_pos=has_pos),
        grid=(b, l // tl),
        in_specs=in_specs,
        out_specs=[pl.BlockSpec((1, tl, d), tok), pl.BlockSpec((1, tl, d), tok),
                   pl.BlockSpec((1, tl, nq), tok)],
        out_shape=[jax.ShapeDtypeStruct((b, l, d), F32), jax.ShapeDtypeStruct((b, l, d), BF16),
                   jax.ShapeDtypeStruct((b, l, nq), BF16)],
        compiler_params=_params(("parallel", "parallel"), 56),
        name="merge",
    )(*args)


RANK_NONE = 1.0e9


def _topk_rank(s, k, exact):
    n = s.shape[0]
    rank = jnp.full(s.shape, RANK_NONE, F32)
    vals = []
    if exact:
        iota = lax.broadcasted_iota(jnp.int32, s.shape, 0).astype(F32)
    for r in range(k):
        m = jnp.max(s, axis=0, keepdims=True)
        hit = s == m
        if exact:
            idx = jnp.min(jnp.where(hit, iota, float(n)), axis=0, keepdims=True)
            hit = iota == idx
        rank = jnp.where(hit, float(r), rank)
        s = jnp.where(hit, NEG_INF, s)
        vals.append(m)
    if exact:
        ties = jnp.zeros((1, s.shape[1]), F32)
    else:
        taken = jnp.sum(jnp.where(rank < RANK_NONE, 1.0, 0.0), axis=0, keepdims=True)
        ties = jnp.where(taken > float(k), 1.0, 0.0)
    return vals, rank, ties


def _select_tile(s1, s2, exact):
    k = PEER_TOPK
    v1, rank1, t1 = _topk_rank(s1, k, exact)
    v2, rank2, t2 = _topk_rank(s2, k, exact)
    v2_16 = jnp.concatenate(v2, axis=0)
    v2_8 = v2_16[0:8]
    v1_hi = jnp.concatenate(v1[8:16], axis=0)
    cand = jnp.concatenate([v1[0] + v2_16] + [v1[a] + v2_8 for a in range(1, 8)] + [v1_hi + v2[0]], axis=0)
    _, rankc, tc = _topk_rank(cand, k, exact)
    sel = jnp.where(rankc < float(k), 1.0, 0.0)
    z = jnp.sum(sel * jnp.exp(cand - cand[0:1]), axis=0, keepdims=True)
    counts = [jnp.sum(sel[0:16], axis=0, keepdims=True)]
    counts += [jnp.sum(sel[8 + 8 * a:16 + 8 * a], axis=0, keepdims=True) for a in range(1, 8)]
    counts += [sel[72 + a:73 + a] for a in range(8)]
    ni = jnp.zeros(s1.shape, F32)
    for a in range(k):
        ni = jnp.where(rank1 == float(a), counts[a], ni)
    e1n = jnp.exp(s1 - v1[0]) / z
    e2 = jnp.exp(s2 - v2[0])
    return e1n, ni, e2.astype(BF16), rank2.astype(BF16), t1 + t2 + tc


def _peersel_body(q_ref, k_ref, e1n_ref, ni_ref, e2_ref, r2_ref, s1_ref, s2_ref):
    nt = (((1,), (1,)), ((), ()))
    s1_ref[...] = lax.dot_general(k_ref[0], q_ref[:, 0:PEER_HALF], nt, preferred_element_type=F32)
    s2_ref[...] = lax.dot_general(k_ref[1], q_ref[:, PEER_HALF:2 * PEER_HALF], nt, preferred_element_type=F32)
    lane = 128
    n_tiles = s1_ref.shape[1] // lane

    def run(cols, exact):
        e1n, ni, e2, r2, ties = _select_tile(s1_ref[:, cols], s2_ref[:, cols], exact)
        e1n_ref[0, :, cols] = e1n
        ni_ref[0, :, cols] = ni
        e2_ref[0, :, cols] = e2
        r2_ref[0, :, cols] = r2
        return ties

    for lt in range(n_tiles):
        cols = slice(lt * lane, (lt + 1) * lane)
        ties = run(cols, exact=False)

        @pl.when(jnp.max(ties) > 0.0)
        def _(cols=cols):
            run(cols, exact=True)


def _peersel(q2, keys_bf, tt):
    t = q2.shape[0]
    out_spec = pl.BlockSpec((1, N_KEYS, tt), lambda ti, h: (h, 0, ti))
    return pl.pallas_call(
        _peersel_body,
        grid=(t // tt, PEER_HEADS),
        in_specs=[pl.BlockSpec((tt, 2 * PEER_HALF), lambda ti, h: (ti, h)),
                  pl.BlockSpec((2, N_KEYS, PEER_HALF), lambda ti, h: (0, 0, 0))],
        out_specs=[out_spec] * 4,
        out_shape=[jax.ShapeDtypeStruct((PEER_HEADS, N_KEYS, t), F32)] * 2
        + [jax.ShapeDtypeStruct((PEER_HEADS, N_KEYS, t), BF16)] * 2,
        scratch_shapes=[pltpu.VMEM((N_KEYS, tt), F32), pltpu.VMEM((N_KEYS, tt), F32)],
        compiler_params=_params(("parallel", "parallel"), 48),
        name="peersel",
    )(q2, keys_bf)


PEER_EB = 2048
PEER_EG = 512


def _gelu_sigmoid(x):
    k1 = 2.0 * math.sqrt(2.0 / math.pi)
    k2 = k1 * 0.044715
    z = x * (-k1 - k2 * (x * x))
    return x * (1.0 / (1.0 + jnp.exp(z)))


def _peermix_body(h2_ref, u_ref, vt_ref, e1n_ref, ni_ref, e2_ref, r2_ref, x1_ref, mod_ref, lg_ref, lb_ref,
                  o_ref, acc_ref, w_ref, act_ref, g_ref, h2t_ref):
    e = pl.program_id(1)
    tt = acc_ref.shape[1]
    n_groups = PEER_EB // PEER_EG
    slabs_per_group = PEER_EG // N_KEYS
    pack = 16
    n_chunks = N_KEYS // pack
    lanes = 256
    zero = jnp.zeros((), BF16)

    @pl.when(e == 0)
    def _():
        acc_ref[...] = jnp.zeros(acc_ref.shape, F32)
        h2t_ref[...] = h2_ref[...].T

    w_ref[0] = jnp.zeros(w_ref.shape[1:], BF16)
    act_ref[1] = jnp.zeros(act_ref.shape[1:], F32)
    g_ref[1] = jnp.zeros(g_ref.shape[1:], BF16)

    def activate(slot):
        for c in range(PEER_EG // pack):
            rows = slice(c * pack, (c + 1) * pack)
            w_ref[slot, rows, :] = _gelu_sigmoid(act_ref[slot, rows, :]).astype(BF16) * g_ref[slot, rows, :]

    def stage(p, cur):
        prev = 1 - cur
        acc_ref[...] += _dot(vt_ref[max(p - 2, 0)], w_ref[cur])
        activate(prev)
        act_ref[cur] = _dot(u_ref[p * PEER_EG:(p + 1) * PEER_EG, :], h2t_ref[...])
        for s2, lh in [(a, b) for a in range(slabs_per_group) for b in range(tt // lanes)]:
            s = p * slabs_per_group + s2
            cols = slice(lh * lanes, (lh + 1) * lanes)
            g = [None] * n_chunks
            for h in range(PEER_HEADS):
                nrow = jnp.broadcast_to(ni_ref[h, s:s + 1, cols], (pack, lanes)).astype(BF16)
                erow = jnp.broadcast_to(e1n_ref[h, s:s + 1, cols], (pack, lanes)).astype(BF16)
                for c in range(n_chunks):
                    rows = slice(c * pack, (c + 1) * pack)
                    term = jnp.where(r2_ref[h, rows, cols] < nrow, e2_ref[h, rows, cols], zero) * erow
                    g[c] = term if g[c] is None else g[c] + term
            for c in range(n_chunks):
                g_ref[cur, s2 * N_KEYS + c * pack:s2 * N_KEYS + (c + 1) * pack, cols] = g[c]

    def pair(i, carry):
        stage(2 * i, 0)
        stage(2 * i + 1, 1)
        return carry

    for i in range(n_groups // 2):
        pair(i, 0)
    acc_ref[...] += _dot(vt_ref[n_groups - 2], w_ref[0])
    activate(1)
    acc_ref[...] += _dot(vt_ref[n_groups - 1], w_ref[1])

    @pl.when(e == pl.num_programs(1) - 1)
    def _():
        d = D_MODEL
        g2 = mod_ref[0][:, 5 * d:6 * d]
        y = acc_ref[...].T
        o_ref[...] = _layer_norm(DN_ALPHA * x1_ref[...] + g2 * y, lg_ref[...], lb_ref[...])


def _peermix(h2, u_bf, vt_bf, e1n, ni, e2, r2, x1, mod3, mod_row_of_tile, lg, lb, tt):
    t, d = h2.shape
    ne = u_bf.shape[0]
    ns = PEER_EB // N_KEYS
    tile = lambda ti, e: (ti, 0)
    const = lambda ti, e: (0, 0)
    return pl.pallas_call(
        _peermix_body,
        grid=(t // tt, ne // PEER_EB),
        in_specs=[pl.BlockSpec((tt, d), tile),
                  pl.BlockSpec((PEER_EB, d), lambda ti, e: (e, 0)),
                  pl.BlockSpec((PEER_EB // PEER_EG, d, PEER_EG), lambda ti, e: (e, 0, 0)),
                  pl.BlockSpec((PEER_HEADS, ns, tt), lambda ti, e: (0, e, ti)),
                  pl.BlockSpec((PEER_HEADS, ns, tt), lambda ti, e: (0, e, ti)),
                  pl.BlockSpec((PEER_HEADS, N_KEYS, tt), lambda ti, e: (0, 0, ti)),
                  pl.BlockSpec((PEER_HEADS, N_KEYS, tt), lambda ti, e: (0, 0, ti)),
                  pl.BlockSpec((tt, d), tile),
                  pl.BlockSpec((1, 1, 6 * d), lambda ti, e: (mod_row_of_tile(ti), 0, 0)),
                  pl.BlockSpec((1, d), const), pl.BlockSpec((1, d), const)],
        out_specs=pl.BlockSpec((tt, d), tile),
        out_shape=jax.ShapeDtypeStruct((t, d), F32),
        scratch_shapes=[pltpu.VMEM((d, tt), F32), pltpu.VMEM((2, PEER_EG, tt), BF16),
                        pltpu.VMEM((2, PEER_EG, tt), F32), pltpu.VMEM((2, PEER_EG, tt), BF16),
                        pltpu.VMEM((d, tt), BF16)],
        compiler_params=_params(("parallel", "arbitrary"), 56),
        name="peermix",
    )(h2, u_bf, vt_bf, e1n, ni, e2, r2, x1, mod3, lg, lb)


def _prep_params(p):
    hd = RNN_HEAD_DIM
    q = dict(p)
    q["w_in_bf"] = p["w_in"].astype(BF16)
    q["wg"] = jnp.transpose(p["rnn_gate_w"], (2, 3, 0, 1, 4)).reshape(N_RNN_HEADS, hd, 4 * hd).astype(BF16)
    q["bg"] = jnp.transpose(p["rnn_gate_b"].reshape(2, 2, N_RNN_HEADS, hd), (2, 0, 1, 3)).reshape(
        N_RNN_HEADS, 1, 4 * hd)
    q["lam"] = jnp.transpose(p["rnn_lambda"].reshape(2, N_RNN_HEADS, hd), (1, 0, 2))
    q["rnn_cb"] = p["rnn_conv_b"][None, :]
    q["hy_cb"] = p["hy_conv_b"][None, :]
    q["wa"] = p["w_branch_a"].astype(BF16)
    q["wb"] = p["w_branch_b"].astype(BF16)
    q["wo"] = p["w_out"].astype(BF16)
    q["wq"] = p["peer_w_query"].astype(BF16)
    q["ln1_g2"] = p["ln1_g"][None, :]
    q["ln1_b2"] = p["ln1_b"][None, :]
    q["ln2_g2"] = p["ln2_g"][None, :]
    q["ln2_b2"] = p["ln2_b"][None, :]
    return q


def _mixer_group(x, pos, mod3, mod_row_of_batch, h0, q, tl, bb, dt):
    b, l, _ = x.shape
    fwd, inv = _dft_matrices(l)
    kr, ki = _hyfilt(l, fwd, q)
    rx2, gg, hy, ga, gb = _inproj(x, pos, mod3, q["w_in_bf"], mod_row_of_batch, tl)
    y3, st = _rglru(rx2.reshape(l, b, D_RNN), h0, q["wg"], q["bg"], q["lam"], q["rnn_conv_w"],
                    q["rnn_cb"], bb)
    z = _hyena(hy, q["hy_conv_w"], q["hy_cb"], kr, ki, q["hy_skip"], fwd, inv, dt)
    x1, h2, qq = _merge(x, pos, mod3, mod_row_of_batch, y3.reshape(l, b * D_RNN), gg, z, ga, gb,
                        q["wa"], q["wb"], q["wo"], q["ln1_g2"], q["ln1_b2"], q["wq"], tl)
    return x1, h2, qq, st


def _group_transposed(v):
    ne, d = v.shape
    return jnp.transpose(v.astype(BF16).reshape(ne // PEER_EG, PEER_EG, d), (0, 2, 1))


def _peer_group(x1, h2, qq, mod3, mod_row_of_tile, q, tt_sel, tt_mix):
    b, l, d = x1.shape
    t = b * l
    e1n, ni, e2, r2 = _peersel(qq.reshape(t, qq.shape[-1]), q["keys_bf"], tt_sel)
    out = _peermix(h2.reshape(t, d), q["u_bf"], q["vt_bf"], e1n, ni, e2, r2, x1.reshape(t, d), mod3,
                   mod_row_of_tile, q["ln2_g2"], q["ln2_b2"], tt_mix)
    return out.reshape(b, l, d)


def _grid_pos_embed(n_tokens):
    rows = n_tokens // GRID_W
    t = np.arange(rows * GRID_W)
    r = (t // GRID_W).astype(np.float32)
    col = (t % GRID_W).astype(np.float32)
    quarter = D_MODEL // 4
    omega = (1.0 / (10000.0 ** (np.arange(quarter, dtype=np.float32) / np.float32(quarter)))).astype(np.float32)
    er = r[:, None] * omega[None, :]
    ec = col[:, None] * omega[None, :]
    return jnp.asarray(np.concatenate([np.sin(er), np.cos(er), np.sin(ec), np.cos(ec)], axis=-1), dtype=F32)


PARAM_NAMES = ("w_ada", "b_ada", "w_in", "rnn_conv_w", "rnn_conv_b", "rnn_gate_w", "rnn_gate_b", "rnn_lambda",
               "hy_conv_w", "hy_conv_b", "hy_ffn_w1", "hy_ffn_b1", "hy_ffn_w2", "hy_ffn_b2", "hy_ffn_w3",
               "hy_ffn_b3", "hy_sin_freq", "hy_skip", "w_branch_a", "w_branch_b", "w_out", "ln1_g", "ln1_b",
               "ln2_g", "ln2_b", "peer_w_query", "peer_sub_keys", "peer_u", "peer_v")


def kernel(x_prompt, x_sample, state_rglru, c, c_ctx, w_ada, b_ada, w_in, rnn_conv_w, rnn_conv_b, rnn_gate_w,
           rnn_gate_b, rnn_lambda, hy_conv_w, hy_conv_b, hy_ffn_w1, hy_ffn_b1, hy_ffn_w2, hy_ffn_b2, hy_ffn_w3,
           hy_ffn_b3, hy_sin_freq, hy_skip, w_branch_a, w_branch_b, w_out, ln1_g, ln1_b, ln2_g, ln2_b,
           peer_w_query, peer_sub_keys, peer_u, peer_v):
    stacked = dict(zip(PARAM_NAMES, (w_ada, b_ada, w_in, rnn_conv_w, rnn_conv_b, rnn_gate_w, rnn_gate_b,
                                     rnn_lambda, hy_conv_w, hy_conv_b, hy_ffn_w1, hy_ffn_b1, hy_ffn_w2,
                                     hy_ffn_b2, hy_ffn_w3, hy_ffn_b3, hy_sin_freq, hy_skip, w_branch_a,
                                     w_branch_b, w_out, ln1_g, ln1_b, ln2_g, ln2_b, peer_w_query,
                                     peer_sub_keys, peer_u, peer_v)))
    depth = w_ada.shape[0]
    bp, lp, d = x_prompt.shape
    bs, ls, _ = x_sample.shape
    assert bs + 1 <= MOD_ROWS
    cond = jnp.zeros((MOD_ROWS, d), F32).at[0].set(c_ctx).at[1:1 + bs].set(c)
    pos = _grid_pos_embed(ls)
    tt_mix = 512
    tt_sel = 512
    sample_tiles_per_batch = ls // tt_mix

    xp, xs = x_prompt, x_sample
    ctx_states = []
    for layer in range(depth):
        q = _prep_params({name: w[layer] for name, w in stacked.items()})
        q["keys_bf"] = q["peer_sub_keys"].astype(BF16)
        q["u_bf"] = q["peer_u"].astype(BF16)
        q["vt_bf"] = _group_transposed(q["peer_v"])
        mod3 = _ada(cond, q["w_ada"], q["b_ada"][None, :]).reshape(MOD_ROWS, 1, 6 * d)

        x1, h2, qq, st = _mixer_group(xp, None, mod3, lambda bi: 0, jnp.zeros((2, bp, D_RNN), F32), q,
                                      tl=lp, bb=32, dt=1024)
        xp = _peer_group(x1, h2, qq, mod3, lambda ti: 0, q, tt_sel, tt_mix)
        ctx_states.append(jnp.transpose(st, (1, 0, 2)))

        pos_l = pos if layer == 0 else None
        h0 = jnp.transpose(state_rglru[:, layer], (1, 0, 2))
        x1, h2, qq, _ = _mixer_group(xs, pos_l, mod3, lambda bi: bi + 1, h0, q, tl=512, bb=8, dt=512)
        xs = _peer_group(x1, h2, qq, mod3, lambda ti: 1 + ti // sample_tiles_per_batch, q, tt_sel, tt_mix)

    new_state = jnp.stack(ctx_states, axis=1).astype(x_prompt.dtype)
    return (xp, xs, new_state)
```

```python
import functools
import math

import numpy as np
import jax
import jax.numpy as jnp
from jax import lax
from jax.experimental import pallas as pl
from jax.experimental.pallas import tpu as pltpu

F32 = jnp.float32
BF16 = jnp.bfloat16

D_MODEL = 1024
D_RNN = 1024
N_RNN_HEADS = 4
RNN_HEAD_DIM = D_RNN // N_RNN_HEADS
RNN_CONV_W = 4
RGLRU_C = 8.0
D_HY = 1024
HY_ORDER = 2
HY_EMB_BANDS = 16
HY_EMB_DIM = 1 + 2 * HY_EMB_BANDS
HY_EMB_PAD = 64
HY_FILTER_HIDDEN = 64
HY_DECAY_TARGET = 1e-2
HY_MIN_DECAY = math.log(HY_DECAY_TARGET) / 1.5
HY_MAX_DECAY = math.log(HY_DECAY_TARGET) / 0.3
GRID_W = 64
N_KEYS = 128
N_EXPERTS = N_KEYS * N_KEYS
PEER_HEADS = 8
PEER_HALF = 128
PEER_TOPK = 16
DEPTH = 1
DN_ALPHA = (2.0 * DEPTH) ** 0.25
LN_EPS = 1e-5

MOD_ROWS = 16
MIB = 1024 * 1024
NEG_INF = float("-inf")


def _params(semantics, vmem_mib):
    return pltpu.CompilerParams(dimension_semantics=semantics, vmem_limit_bytes=vmem_mib * MIB)


def _gelu(x):
    return jax.nn.gelu(x, approximate=True)


def _dot(a, b):
    return jnp.dot(a, b, preferred_element_type=F32)


def _dot_f32(a, b):
    return jnp.dot(a, b, preferred_element_type=F32, precision=lax.Precision.HIGHEST)


def _ada_body(c_ref, w_ref, b_ref, o_ref):
    c = c_ref[...]
    o_ref[...] = _dot_f32(c * jax.nn.sigmoid(c), w_ref[...]) + b_ref[...]


def _ada(cond, w_ada, b_ada):
    n = w_ada.shape[1]
    tn = 1024
    return pl.pallas_call(
        _ada_body,
        grid=(n // tn,),
        in_specs=[pl.BlockSpec((MOD_ROWS, D_MODEL), lambda j: (0, 0)),
                  pl.BlockSpec((D_MODEL, tn), lambda j: (0, j)),
                  pl.BlockSpec((1, tn), lambda j: (0, j))],
        out_specs=pl.BlockSpec((MOD_ROWS, tn), lambda j: (0, j)),
        out_shape=jax.ShapeDtypeStruct((MOD_ROWS, n), F32),
        compiler_params=_params(("arbitrary",), 32),
        name="ada",
    )(cond, w_ada, b_ada)


def _inproj_body(*refs, has_pos):
    if has_pos:
        x_ref, pos_ref, mod_ref, w_ref, rx_ref, gg_ref, hy_ref, ga_ref, gb_ref = refs
    else:
        x_ref, mod_ref, w_ref, rx_ref, gg_ref, hy_ref, ga_ref, gb_ref = refs
    x = x_ref[0]
    if has_pos:
        x = x + pos_ref[...]
    mod = mod_ref[0]
    sh1 = mod[:, 0:D_MODEL]
    sc1 = mod[:, D_MODEL:2 * D_MODEL]
    h = (x * (1.0 + sc1) + sh1).astype(BF16)
    o = 0
    rx_ref[...] = _dot(h, w_ref[:, o:o + D_RNN])
    o += D_RNN
    gg_ref[0] = _gelu(_dot(h, w_ref[:, o:o + D_RNN])).astype(BF16)
    o += D_RNN
    for j in range(3):
        hy_ref[0, :, j * D_HY:(j + 1) * D_HY] = _dot(h, w_ref[:, o:o + D_HY])
        o += D_HY
    ga_ref[0] = jax.nn.sigmoid(_dot(h, w_ref[:, o:o + D_MODEL])).astype(BF16)
    o += D_MODEL
    gb_ref[0] = jax.nn.sigmoid(_dot(h, w_ref[:, o:o + D_MODEL])).astype(BF16)


def _inproj(x, pos, mod3, w_in_bf, mod_row_of_batch, tl):
    b, l, _ = x.shape
    d_in = w_in_bf.shape[1]
    has_pos = pos is not None
    in_specs = [pl.BlockSpec((1, tl, D_MODEL), lambda bi, ti: (bi, ti, 0))]
    args = [x]
    if has_pos:
        in_specs.append(pl.BlockSpec((tl, D_MODEL), lambda bi, ti: (ti, 0)))
        args.append(pos)
    in_specs += [pl.BlockSpec((1, 1, 6 * D_MODEL), lambda bi, ti: (mod_row_of_batch(bi), 0, 0)),
                 pl.BlockSpec((D_MODEL, d_in), lambda bi, ti: (0, 0), pipeline_mode=pl.Buffered(1))]
    args += [mod3, w_in_bf]
    tok = lambda bi, ti: (bi, ti, 0)
    out_specs = [pl.BlockSpec((tl, D_RNN), lambda bi, ti: (ti, bi)),
                 pl.BlockSpec((1, tl, D_RNN), tok),
                 pl.BlockSpec((1, tl, 3 * D_HY), tok),
                 pl.BlockSpec((1, tl, D_MODEL), tok),
                 pl.BlockSpec((1, tl, D_MODEL), tok)]
    out_shape = [jax.ShapeDtypeStruct((l, b * D_RNN), F32),
                 jax.ShapeDtypeStruct((b, l, D_RNN), BF16),
                 jax.ShapeDtypeStruct((b, l, 3 * D_HY), F32),
                 jax.ShapeDtypeStruct((b, l, D_MODEL), BF16),
                 jax.ShapeDtypeStruct((b, l, D_MODEL), BF16)]
    return pl.pallas_call(
        functools.partial(_inproj_body, has_pos=has_pos),
        grid=(b, l // tl),
        in_specs=in_specs, out_specs=out_specs, out_shape=out_shape,
        compiler_params=_params(("parallel", "parallel"), 56),
        name="inproj",
    )(*args)


def _softplus(x):
    return jnp.maximum(x, 0.0) + jnp.log(1.0 + jnp.exp(-jnp.abs(x)))


def _rglru_body(rx_ref, h0_ref, wg_ref, bg_ref, lam_ref, cw_ref, cb_ref, y_ref, st_ref, a_s, u_s,
                *, seq, bb, tc):
    hd = RNN_HEAD_DIM
    nchunks = seq // tc
    cw = cw_ref[...]
    cb = cb_ref[...]
    for d in range(2):
        coef = -RGLRU_C * _softplus(-lam_ref[0, d:d + 1, :])
        wd = wg_ref[0, :, 2 * d * hd:2 * (d + 1) * hd]
        bd = bg_ref[0, :, 2 * d * hd:2 * (d + 1) * hd]

        def chunk(ci, h, d=d, coef=coef, wd=wd, bd=bd):
            c = ci if d == 0 else nchunks - 1 - ci
            t0 = pl.multiple_of(c * tc, tc)
            lo_ok = jnp.where(c > 0, 1.0, 0.0)
            hi_ok = jnp.where(c < nchunks - 1, 1.0, 0.0)
            lo = rx_ref[pl.ds(jnp.maximum(t0 - 2, 0), 2)] * lo_ok
            hi = rx_ref[pl.ds(jnp.minimum(t0 + tc, seq - 1), 1)] * hi_ok
            xe = jnp.concatenate([lo, rx_ref[pl.ds(t0, tc)], hi], axis=0)
            xc = cb[None] + sum(cw[k:k + 1][None] * xe[k:k + tc] for k in range(RNN_CONV_W))
            xc2 = xc.reshape(tc * bb, hd)
            g = _dot(xc2.astype(BF16), wd) + bd
            r = jax.nn.sigmoid(g[:, :hd])
            i = jax.nn.sigmoid(g[:, hd:])
            a = jnp.exp(coef * r)
            u = jnp.sqrt(1.0 - a * a) * (i * xc2)
            a_s[...] = a.reshape(tc, bb, hd)
            u_s[...] = u.reshape(tc, bb, hd)

            def step(j, h):
                tau = j if d == 0 else tc - 1 - j
                h = a_s[tau] * h + u_s[tau]
                if d == 0:
                    y_ref[t0 + tau] = h
                else:
                    y_ref[t0 + tau] = y_ref[t0 + tau] + h
                return h

            return lax.fori_loop(0, tc, step, h, unroll=8)

        h_fin = lax.fori_loop(0, nchunks, chunk, h0_ref[d])
        st_ref[d] = h_fin


def _rglru(rx3, h0, wg, bg, lam, cw, cb, bb):
    seq, b, _ = rx3.shape
    hd = RNN_HEAD_DIM
    tc = max(8, 512 // bb)
    blk = lambda bi, hi: (0, bi, hi)
    return pl.pallas_call(
        functools.partial(_rglru_body, seq=seq, bb=bb, tc=tc),
        grid=(b // bb, N_RNN_HEADS),
        in_specs=[pl.BlockSpec((seq, bb, hd), blk),
                  pl.BlockSpec((2, bb, hd), blk),
                  pl.BlockSpec((1, hd, 4 * hd), lambda bi, hi: (hi, 0, 0)),
                  pl.BlockSpec((1, 1, 4 * hd), lambda bi, hi: (hi, 0, 0)),
                  pl.BlockSpec((1, 2, hd), lambda bi, hi: (hi, 0, 0)),
                  pl.BlockSpec((RNN_CONV_W, hd), lambda bi, hi: (0, hi)),
                  pl.BlockSpec((1, hd), lambda bi, hi: (0, hi))],
        out_specs=[pl.BlockSpec((seq, bb, hd), blk),
                   pl.BlockSpec((2, bb, hd), blk)],
        out_shape=[jax.ShapeDtypeStruct((seq, b, D_RNN), F32),
                   jax.ShapeDtypeStruct((2, b, D_RNN), F32)],
        scratch_shapes=[pltpu.VMEM((tc, bb, hd), F32), pltpu.VMEM((tc, bb, hd), F32)],
        compiler_params=_params(("parallel", "parallel"), 56),
        name="rglru",
    )(rx3, h0, wg, bg, lam, cw, cb)


def _dft_matrices(seq):
    k = np.arange(seq, dtype=np.float64)[:, None]
    s = np.arange(seq, dtype=np.float64)[None, :]
    ang = np.pi * k * s / seq
    top = np.cos(ang)
    bot = -np.sin(ang)
    bot[0, :] = np.where(np.arange(seq) % 2 == 0, 1.0, -1.0)
    fwd = np.concatenate([top, bot], axis=0)
    return jnp.asarray(fwd, dtype=BF16), jnp.asarray(fwd.T, dtype=BF16)


def _filter_features(seq):
    t = np.arange(seq, dtype=np.float32)
    t_norm = t / np.float32(max(seq - 1, 1))
    w = (np.float32(2.0 * math.pi) * t / np.float32(seq)).astype(np.float32)
    bands = np.linspace(1e-4, HY_EMB_BANDS - 1, HY_EMB_BANDS, dtype=np.float32)
    fw = w[:, None] * bands[None, :]
    z = np.concatenate([t_norm[:, None], np.cos(fw), -np.sin(fw)], axis=-1).astype(np.float32)
    zp = np.zeros((seq, HY_EMB_PAD), np.float32)
    zp[:, :HY_EMB_DIM] = z
    deltas = np.abs(np.linspace(HY_MIN_DECAY, HY_MAX_DECAY, D_HY, dtype=np.float32))
    return jnp.asarray(zp), jnp.asarray(t_norm[:, None]), jnp.asarray(deltas[None, :])


def _dot_split(f_bf, x):
    hi = x.astype(BF16)
    lo = (x - hi.astype(F32)).astype(BF16)
    return _dot(f_bf, hi) + _dot(f_bf, lo)


def _hyfilt_body(z_ref, tn_ref, dl_ref, w1_ref, b1_ref, w2_ref, b2_ref, fr_ref, w3f_ref, b3f_ref,
                 w3b_ref, b3b_ref, f_ref, kr_ref, ki_ref, *, seq):
    freq = fr_ref[...]
    hid = jnp.sin(freq * (_dot_f32(z_ref[...], w1_ref[...]) + b1_ref[...]))
    hid = jnp.sin(freq * (_dot_f32(hid, w2_ref[...]) + b2_ref[...]))
    decay = jnp.exp(-tn_ref[...] * dl_ref[...])
    ff = (_dot_f32(hid, w3f_ref[...]) + b3f_ref[...]) * decay
    fb = (_dot_f32(hid, w3b_ref[...]) + b3b_ref[...]) * decay
    row = lax.broadcasted_iota(jnp.int32, ff.shape, 0)
    fb = jnp.where(row == 0, 0.0, fb)
    s = ff + fb
    dm = ff - fb
    sign = jnp.where((row & 1) == 0, 1.0, -1.0)
    nyq = jnp.sum(s * sign, axis=0, keepdims=True)
    kr = _dot_split(f_ref[0:seq, :], s)
    ki = _dot_split(f_ref[seq:2 * seq, :], dm)
    scale = jnp.where(row == 0, 0.5 / seq, 1.0 / seq)
    kr_ref[0] = kr * scale
    ki_ref[0] = jnp.where(row == 0, nyq, ki) * scale


def _hyfilt(seq, fwd, p):
    z, tn, dl = _filter_features(seq)
    dt = 512
    nd = D_HY // dt
    hidn = HY_FILTER_HIDDEN
    const = lambda o, j: (0, 0)
    w3 = p["hy_ffn_w3"]
    b3 = p["hy_ffn_b3"][None, :]
    in_specs = [pl.BlockSpec((seq, HY_EMB_PAD), const),
                pl.BlockSpec((seq, 1), const),
                pl.BlockSpec((1, dt), lambda o, j: (0, j)),
                pl.BlockSpec((HY_EMB_PAD, hidn), const),
                pl.BlockSpec((1, hidn), const),
                pl.BlockSpec((hidn, hidn), const),
                pl.BlockSpec((1, hidn), const),
                pl.BlockSpec((1, hidn), const),
                pl.BlockSpec((hidn, dt), lambda o, j: (0, o * nd + j)),
                pl.BlockSpec((1, dt), lambda o, j: (0, o * nd + j)),
                pl.BlockSpec((hidn, dt), lambda o, j: (0, (HY_ORDER + o) * nd + j)),
                pl.BlockSpec((1, dt), lambda o, j: (0, (HY_ORDER + o) * nd + j)),
                pl.BlockSpec((2 * seq, seq), const)]
    w1p = jnp.zeros((HY_EMB_PAD, hidn), F32).at[:HY_EMB_DIM].set(p["hy_ffn_w1"])
    out_spec = pl.BlockSpec((1, seq, dt), lambda o, j: (o, 0, j))
    return pl.pallas_call(
        functools.partial(_hyfilt_body, seq=seq),
        grid=(HY_ORDER, nd),
        in_specs=in_specs,
        out_specs=[out_spec, out_spec],
        out_shape=[jax.ShapeDtypeStruct((HY_ORDER, seq, D_HY), F32)] * 2,
        compiler_params=_params(("parallel", "parallel"), 48),
        name="hyfilt",
    )(z, tn, dl, w1p, p["hy_ffn_b1"][None, :], p["hy_ffn_w2"], p["hy_ffn_b2"][None, :],
      p["hy_sin_freq"][None, :], w3, b3, w3, b3, fwd)


def _hyena_body(x1_ref, x2_ref, v_ref, w1_ref, w2_ref, wv_ref, b1_ref, b2_ref, bv_ref,
                kr_ref, ki_ref, sk_ref, f_ref, ft_ref, o_ref, *, seq, lanes):
    dt = v_ref.shape[2]
    row = lax.broadcasted_iota(jnp.int32, (seq, lanes), 0)
    first = row == 0
    last = row == seq - 1

    def chain(cols):
        def conv3(x_ref, w_ref, b_ref):
            x = x_ref[0, :, cols]
            xm = jnp.where(first, 0.0, pltpu.roll(x, 1, 0))
            xp = jnp.where(last, 0.0, pltpu.roll(x, seq - 1, 0))
            return w_ref[0:1, cols] * xm + w_ref[1:2, cols] * x + w_ref[2:3, cols] * xp + b_ref[:, cols]

        def long_conv(u, o):
            kr = kr_ref[o, :, cols]
            ki = ki_ref[o, :, cols]
            kiz = jnp.where(first, 0.0, ki)
            krb = jnp.where(first, ki, kr)
            uf = _dot(f_ref[...], u.astype(BF16))
            top = uf[:seq]
            bot = uf[seq:]
            y = jnp.concatenate([top * kr - bot * kiz, top * kiz + bot * krb], axis=0).astype(BF16)
            return _dot(ft_ref[...], y) + u * sk_ref[o:o + 1, cols]

        z = conv3(x1_ref, w1_ref, b1_ref) * long_conv(conv3(v_ref, wv_ref, bv_ref), 0)
        z = conv3(x2_ref, w2_ref, b2_ref) * long_conv(z, 1)
        o_ref[0, :, cols] = z.astype(BF16)

    for c in range(dt // lanes):
        chain(slice(c * lanes, (c + 1) * lanes))


def _hyena(hy, cw, cb, kr, ki, skip, fwd, inv, dt, lanes):
    b, seq, _ = hy.shape
    nd = D_HY // dt
    const = lambda j, bi: (0, 0)
    part = lambda k: (lambda j, bi: (bi, 0, k * nd + j))
    wpart = lambda k: (lambda j, bi: (0, k * nd + j))
    once = pl.Buffered(1)
    in_specs = ([pl.BlockSpec((1, seq, dt), part(k)) for k in range(3)]
                + [pl.BlockSpec((3, dt), wpart(k)) for k in range(3)]
                + [pl.BlockSpec((1, dt), wpart(k)) for k in range(3)]
                + [pl.BlockSpec((HY_ORDER, seq, dt), lambda j, bi: (0, 0, j), pipeline_mode=once),
                   pl.BlockSpec((HY_ORDER, seq, dt), lambda j, bi: (0, 0, j), pipeline_mode=once),
                   pl.BlockSpec((HY_ORDER, dt), lambda j, bi: (0, j)),
                   pl.BlockSpec((2 * seq, seq), const, pipeline_mode=once),
                   pl.BlockSpec((seq, 2 * seq), const, pipeline_mode=once)])
    return pl.pallas_call(
        functools.partial(_hyena_body, seq=seq, lanes=min(dt, lanes)),
        grid=(nd, b),
        in_specs=in_specs,
        out_specs=pl.BlockSpec((1, seq, dt), lambda j, bi: (bi, 0, j)),
        out_shape=jax.ShapeDtypeStruct((b, seq, D_HY), BF16),
        compiler_params=_params(("parallel", "parallel"), 56),
        name="hyena",
    )(hy, hy, hy, cw, cw, cw, cb, cb, cb, kr, ki, skip, fwd, inv)


def _layer_norm(x, g, b):
    mu = jnp.mean(x, axis=-1, keepdims=True)
    xc = x - mu
    var = jnp.mean(xc * xc, axis=-1, keepdims=True)
    return xc * lax.rsqrt(var + LN_EPS) * g + b


def _merge_body(*refs, has_pos):
    if has_pos:
        (x_ref, pos_ref, mod_ref, yr_ref, gg_ref, z_ref, ga_ref, gb_ref, wa_ref, wb_ref, wo_ref,
         lg_ref, lb_ref, wq_ref, x1_ref, h2_ref, q_ref) = refs
    else:
        (x_ref, mod_ref, yr_ref, gg_ref, z_ref, ga_ref, gb_ref, wa_ref, wb_ref, wo_ref,
         lg_ref, lb_ref, wq_ref, x1_ref, h2_ref, q_ref) = refs
    d = D_MODEL
    mod = mod_ref[0]
    g1 = mod[:, 2 * d:3 * d]
    sh2 = mod[:, 3 * d:4 * d]
    sc2 = mod[:, 4 * d:5 * d]
    ya = _dot((yr_ref[...].astype(BF16) * gg_ref[0]), wa_ref[...])
    yb = _dot(z_ref[0], wb_ref[...])
    m = ga_ref[0].astype(F32) * ya + gb_ref[0].astype(F32) * yb
    y = _dot(m.astype(BF16), wo_ref[...])
    x = x_ref[0]
    if has_pos:
        x = x + pos_ref[...]
    x1 = _layer_norm(DN_ALPHA * x + g1 * y, lg_ref[...], lb_ref[...])
    x1_ref[0] = x1
    h2 = (x1 * (1.0 + sc2) + sh2).astype(BF16)
    h2_ref[0] = h2
    q_ref[0] = _dot(h2, wq_ref[...]).astype(BF16)


def _merge(x, pos, mod3, mod_row_of_batch, yr2, gg, z, ga, gb, wa, wb, wo, lg, lb, wq, tl):
    b, l, d = x.shape
    nq = wq.shape[1]
    has_pos = pos is not None
    tok = lambda bi, ti: (bi, ti, 0)
    const = lambda bi, ti: (0, 0)
    in_specs = [pl.BlockSpec((1, tl, d), tok)]
    args = [x]
    if has_pos:
        in_specs.append(pl.BlockSpec((tl, d), lambda bi, ti: (ti, 0)))
        args.append(pos)
    in_specs += [pl.BlockSpec((1, 1, 6 * d), lambda bi, ti: (mod_row_of_batch(bi), 0, 0)),
                 pl.BlockSpec((tl, D_RNN), lambda bi, ti: (ti, bi)),
                 pl.BlockSpec((1, tl, d), tok), pl.BlockSpec((1, tl, d), tok),
                 pl.BlockSpec((1, tl, d), tok), pl.BlockSpec((1, tl, d), tok),
                 pl.BlockSpec((d, d), const), pl.BlockSpec((d, d), const), pl.BlockSpec((d, d), const),
                 pl.BlockSpec((1, d), const), pl.BlockSpec((1, d), const),
                 pl.BlockSpec((d, nq), const)]
    args += [mod3, yr2, gg, z, ga, gb, wa, wb, wo, lg, lb, wq]
    return pl.pallas_call(
        functools.partial(_merge_body, has_pos=has_pos),
        grid=(b, l // tl),
        in_specs=in_specs,
        out_specs=[pl.BlockSpec((1, tl, d), tok), pl.BlockSpec((1, tl, d), tok),
                   pl.BlockSpec((1, tl, nq), tok)],
        out_shape=[jax.ShapeDtypeStruct((b, l, d), F32), jax.ShapeDtypeStruct((b, l, d), BF16),
                   jax.ShapeDtypeStruct((b, l, nq), BF16)],
        compiler_params=_params(("parallel", "parallel"), 56),
        name="merge",
    )(*args)


RANK_NONE = 1.0e9


def _topk_rank(s, k, exact, want_rank=True):
    n = s.shape[0]
    s0 = s
    rank = jnp.full(s.shape, RANK_NONE, F32) if want_rank else None
    vals = []
    if exact:
        iota = lax.broadcasted_iota(jnp.int32, s.shape, 0).astype(F32)
    for r in range(k):
        m = jnp.max(s, axis=0, keepdims=True)
        hit = s == m
        if exact:
            idx = jnp.min(jnp.where(hit, iota, float(n)), axis=0, keepdims=True)
            hit = iota == idx
        if want_rank:
            rank = jnp.where(hit, float(r), rank)
        s = jnp.where(hit, NEG_INF, s)
        vals.append(m)
    taken = jnp.where(s != s0, 1.0, 0.0)
    if exact:
        ties = jnp.zeros((1, s.shape[1]), F32)
    else:
        ties = jnp.where(jnp.sum(taken, axis=0, keepdims=True) > float(k), 1.0, 0.0)
    return vals, rank, taken, ties


def _select_tile(s1, s2, exact):
    k = PEER_TOPK
    v1, rank1, _, t1 = _topk_rank(s1, k, exact, want_rank=exact)
    v2, rank2, _, t2 = _topk_rank(s2, k, exact)
    v2_16 = jnp.concatenate(v2, axis=0)
    v2_8 = v2_16[0:8]
    v1_hi = jnp.concatenate(v1[8:16], axis=0)
    cand = jnp.concatenate([v1[0] + v2_16] + [v1[a] + v2_8 for a in range(1, 8)] + [v1_hi + v2[0]], axis=0)
    _, _, sel, tc = _topk_rank(cand, k, exact, want_rank=False)
    z = jnp.sum(sel * jnp.exp(cand - cand[0:1]), axis=0, keepdims=True)
    counts = [jnp.sum(sel[0:16], axis=0, keepdims=True)]
    counts += [jnp.sum(sel[8 + 8 * a:16 + 8 * a], axis=0, keepdims=True) for a in range(1, 8)]
    counts += [sel[72 + a:73 + a] for a in range(8)]
    ni = jnp.zeros(s1.shape, F32)
    for a in range(k):
        is_a = (rank1 == float(a)) if exact else (s1 == v1[a])
        ni = jnp.where(is_a, counts[a], ni)
    e1n = jnp.exp(s1 - v1[0]) / z
    e2 = jnp.exp(s2 - v2[0])
    return e1n, ni, e2.astype(BF16), rank2.astype(BF16), t1 + t2 + tc


def _peersel_body(q_ref, k_ref, e1n_ref, ni_ref, e2_ref, r2_ref, s1_ref, s2_ref):
    nt = (((1,), (1,)), ((), ()))
    s1_ref[...] = lax.dot_general(k_ref[0], q_ref[:, 0:PEER_HALF], nt, preferred_element_type=F32)
    s2_ref[...] = lax.dot_general(k_ref[1], q_ref[:, PEER_HALF:2 * PEER_HALF], nt, preferred_element_type=F32)
    lane = 128
    n_tiles = s1_ref.shape[1] // lane

    def run(cols, exact):
        e1n, ni, e2, r2, ties = _select_tile(s1_ref[:, cols], s2_ref[:, cols], exact)
        e1n_ref[0, :, cols] = e1n
        ni_ref[0, :, cols] = ni
        e2_ref[0, :, cols] = e2
        r2_ref[0, :, cols] = r2
        return ties

    tiles = [slice(lt * lane, (lt + 1) * lane) for lt in range(n_tiles)]
    ties = [run(cols, exact=False) for cols in tiles]
    for cols, t in zip(tiles, ties):
        @pl.when(jnp.max(t) > 0.0)
        def _(cols=cols):
            run(cols, exact=True)


def _peersel(q2, keys_bf, tt):
    t = q2.shape[0]
    out_spec = pl.BlockSpec((1, N_KEYS, tt), lambda ti, h: (h, 0, ti))
    return pl.pallas_call(
        _peersel_body,
        grid=(t // tt, PEER_HEADS),
        in_specs=[pl.BlockSpec((tt, 2 * PEER_HALF), lambda ti, h: (ti, h)),
                  pl.BlockSpec((2, N_KEYS, PEER_HALF), lambda ti, h: (0, 0, 0))],
        out_specs=[out_spec] * 4,
        out_shape=[jax.ShapeDtypeStruct((PEER_HEADS, N_KEYS, t), F32)] * 2
        + [jax.ShapeDtypeStruct((PEER_HEADS, N_KEYS, t), BF16)] * 2,
        scratch_shapes=[pltpu.VMEM((N_KEYS, tt), F32), pltpu.VMEM((N_KEYS, tt), F32)],
        compiler_params=_params(("parallel", "parallel"), 48),
        name="peersel",
    )(q2, keys_bf)


PEER_EB = 2048
PEER_EG = 512


def _gelu_sigmoid(x):
    k1 = 2.0 * math.sqrt(2.0 / math.pi)
    k2 = k1 * 0.044715
    z = x * (-k1 - k2 * (x * x))
    return x * (1.0 / (1.0 + jnp.exp(z)))


def _peermix_body(h2_ref, u_ref, vt_ref, e1n_ref, ni_ref, e2_ref, r2_ref, x1_ref, mod_ref, lg_ref, lb_ref,
                  o_ref, acc_ref, w_ref, act_ref, g_ref, h2t_ref):
    e = pl.program_id(1)
    tt = acc_ref.shape[1]
    n_groups = PEER_EB // PEER_EG
    slabs_per_group = PEER_EG // N_KEYS
    pack = 16
    n_chunks = N_KEYS // pack
    lanes = 256
    zero = jnp.zeros((), BF16)

    @pl.when(e == 0)
    def _():
        acc_ref[...] = jnp.zeros(acc_ref.shape, F32)
        h2t_ref[...] = h2_ref[...].T

    act_ref[...] = _dot(u_ref[...], h2t_ref[...])
    for s, lh in [(a, b) for a in range(PEER_EB // N_KEYS) for b in range(tt // lanes)]:
        cols = slice(lh * lanes, (lh + 1) * lanes)
        g = [None] * n_chunks
        for h in range(PEER_HEADS):
            nrow = jnp.broadcast_to(ni_ref[h, s:s + 1, cols], (pack, lanes)).astype(BF16)
            erow = jnp.broadcast_to(e1n_ref[h, s:s + 1, cols], (pack, lanes)).astype(BF16)
            for c in range(n_chunks):
                rows = slice(c * pack, (c + 1) * pack)
                term = jnp.where(r2_ref[h, rows, cols] < nrow, e2_ref[h, rows, cols], zero) * erow
                g[c] = term if g[c] is None else g[c] + term
        for c in range(n_chunks):
            g_ref[s * N_KEYS + c * pack:s * N_KEYS + (c + 1) * pack, cols] = g[c]
    for p in range(n_groups):
        for c in range(PEER_EG // pack):
            rows = slice(p * PEER_EG + c * pack, p * PEER_EG + (c + 1) * pack)
            w_ref[rows, :] = _gelu_sigmoid(act_ref[rows, :]).astype(BF16) * g_ref[rows, :]
        acc_ref[...] += _dot(vt_ref[p], w_ref[p * PEER_EG:(p + 1) * PEER_EG, :])

    @pl.when(e == pl.num_programs(1) - 1)
    def _():
        d = D_MODEL
        g2 = mod_ref[0][:, 5 * d:6 * d]
        y = acc_ref[...].T
        o_ref[...] = _layer_norm(DN_ALPHA * x1_ref[...] + g2 * y, lg_ref[...], lb_ref[...])


def _peermix(h2, u_bf, vt_bf, e1n, ni, e2, r2, x1, mod3, mod_row_of_tile, lg, lb, tt):
    t, d = h2.shape
    ne = u_bf.shape[0]
    ns = PEER_EB // N_KEYS
    tile = lambda ti, e: (ti, 0)
    const = lambda ti, e: (0, 0)
    return pl.pallas_call(
        _peermix_body,
        grid=(t // tt, ne // PEER_EB),
        in_specs=[pl.BlockSpec((tt, d), tile),
                  pl.BlockSpec((PEER_EB, d), lambda ti, e: (e, 0)),
                  pl.BlockSpec((PEER_EB // PEER_EG, d, PEER_EG), lambda ti, e: (e, 0, 0)),
                  pl.BlockSpec((PEER_HEADS, ns, tt), lambda ti, e: (0, e, ti)),
                  pl.BlockSpec((PEER_HEADS, ns, tt), lambda ti, e: (0, e, ti)),
                  pl.BlockSpec((PEER_HEADS, N_KEYS, tt), lambda ti, e: (0, 0, ti)),
                  pl.BlockSpec((PEER_HEADS, N_KEYS, tt), lambda ti, e: (0, 0, ti)),
                  pl.BlockSpec((tt, d), tile),
                  pl.BlockSpec((1, 1, 6 * d), lambda ti, e: (mod_row_of_tile(ti), 0, 0)),
                  pl.BlockSpec((1, d), const), pl.BlockSpec((1, d), const)],
        out_specs=pl.BlockSpec((tt, d), tile),
        out_shape=jax.ShapeDtypeStruct((t, d), F32),
        scratch_shapes=[pltpu.VMEM((d, tt), F32), pltpu.VMEM((PEER_EB, tt), BF16),
                        pltpu.VMEM((PEER_EB, tt), F32), pltpu.VMEM((PEER_EB, tt), BF16),
                        pltpu.VMEM((d, tt), BF16)],
        compiler_params=_params(("parallel", "arbitrary"), 56),
        name="peermix",
    )(h2, u_bf, vt_bf, e1n, ni, e2, r2, x1, mod3, lg, lb)


def _prep_params(p):
    hd = RNN_HEAD_DIM
    q = dict(p)
    q["w_in_bf"] = p["w_in"].astype(BF16)
    q["wg"] = jnp.transpose(p["rnn_gate_w"], (2, 3, 0, 1, 4)).reshape(N_RNN_HEADS, hd, 4 * hd).astype(BF16)
    q["bg"] = jnp.transpose(p["rnn_gate_b"].reshape(2, 2, N_RNN_HEADS, hd), (2, 0, 1, 3)).reshape(
        N_RNN_HEADS, 1, 4 * hd)
    q["lam"] = jnp.transpose(p["rnn_lambda"].reshape(2, N_RNN_HEADS, hd), (1, 0, 2))
    q["rnn_cb"] = p["rnn_conv_b"][None, :]
    q["hy_cb"] = p["hy_conv_b"][None, :]
    q["wa"] = p["w_branch_a"].astype(BF16)
    q["wb"] = p["w_branch_b"].astype(BF16)
    q["wo"] = p["w_out"].astype(BF16)
    q["wq"] = p["peer_w_query"].astype(BF16)
    q["ln1_g2"] = p["ln1_g"][None, :]
    q["ln1_b2"] = p["ln1_b"][None, :]
    q["ln2_g2"] = p["ln2_g"][None, :]
    q["ln2_b2"] = p["ln2_b"][None, :]
    return q


def _mixer_group(x, pos, mod3, mod_row_of_batch, h0, q, tl, bb, dt, hy_lanes):
    b, l, _ = x.shape
    fwd, inv = _dft_matrices(l)
    kr, ki = _hyfilt(l, fwd, q)
    rx2, gg, hy, ga, gb = _inproj(x, pos, mod3, q["w_in_bf"], mod_row_of_batch, tl)
    y3, st = _rglru(rx2.reshape(l, b, D_RNN), h0, q["wg"], q["bg"], q["lam"], q["rnn_conv_w"],
                    q["rnn_cb"], bb)
    z = _hyena(hy, q["hy_conv_w"], q["hy_cb"], kr, ki, q["hy_skip"], fwd, inv, dt, hy_lanes)
    x1, h2, qq = _merge(x, pos, mod3, mod_row_of_batch, y3.reshape(l, b * D_RNN), gg, z, ga, gb,
                        q["wa"], q["wb"], q["wo"], q["ln1_g2"], q["ln1_b2"], q["wq"], tl)
    return x1, h2, qq, st


def _group_transposed(v):
    ne, d = v.shape
    return jnp.transpose(v.astype(BF16).reshape(ne // PEER_EG, PEER_EG, d), (0, 2, 1))


def _peer_group(x1, h2, qq, mod3, mod_row_of_tile, q, tt_sel, tt_mix):
    b, l, d = x1.shape
    t = b * l
    e1n, ni, e2, r2 = _peersel(qq.reshape(t, qq.shape[-1]), q["keys_bf"], tt_sel)
    out = _peermix(h2.reshape(t, d), q["u_bf"], q["vt_bf"], e1n, ni, e2, r2, x1.reshape(t, d), mod3,
                   mod_row_of_tile, q["ln2_g2"], q["ln2_b2"], tt_mix)
    return out.reshape(b, l, d)


def _grid_pos_embed(n_tokens):
    rows = n_tokens // GRID_W
    t = np.arange(rows * GRID_W)
    r = (t // GRID_W).astype(np.float32)
    col = (t % GRID_W).astype(np.float32)
    quarter = D_MODEL // 4
    omega = (1.0 / (10000.0 ** (np.arange(quarter, dtype=np.float32) / np.float32(quarter)))).astype(np.float32)
    er = r[:, None] * omega[None, :]
    ec = col[:, None] * omega[None, :]
    return jnp.asarray(np.concatenate([np.sin(er), np.cos(er), np.sin(ec), np.cos(ec)], axis=-1), dtype=F32)


PARAM_NAMES = ("w_ada", "b_ada", "w_in", "rnn_conv_w", "rnn_conv_b", "rnn_gate_w", "rnn_gate_b", "rnn_lambda",
               "hy_conv_w", "hy_conv_b", "hy_ffn_w1", "hy_ffn_b1", "hy_ffn_w2", "hy_ffn_b2", "hy_ffn_w3",
               "hy_ffn_b3", "hy_sin_freq", "hy_skip", "w_branch_a", "w_branch_b", "w_out", "ln1_g", "ln1_b",
               "ln2_g", "ln2_b", "peer_w_query", "peer_sub_keys", "peer_u", "peer_v")


def kernel(x_prompt, x_sample, state_rglru, c, c_ctx, w_ada, b_ada, w_in, rnn_conv_w, rnn_conv_b, rnn_gate_w,
           rnn_gate_b, rnn_lambda, hy_conv_w, hy_conv_b, hy_ffn_w1, hy_ffn_b1, hy_ffn_w2, hy_ffn_b2, hy_ffn_w3,
           hy_ffn_b3, hy_sin_freq, hy_skip, w_branch_a, w_branch_b, w_out, ln1_g, ln1_b, ln2_g, ln2_b,
           peer_w_query, peer_sub_keys, peer_u, peer_v):
    stacked = dict(zip(PARAM_NAMES, (w_ada, b_ada, w_in, rnn_conv_w, rnn_conv_b, rnn_gate_w, rnn_gate_b,
                                     rnn_lambda, hy_conv_w, hy_conv_b, hy_ffn_w1, hy_ffn_b1, hy_ffn_w2,
                                     hy_ffn_b2, hy_ffn_w3, hy_ffn_b3, hy_sin_freq, hy_skip, w_branch_a,
                                     w_branch_b, w_out, ln1_g, ln1_b, ln2_g, ln2_b, peer_w_query,
                                     peer_sub_keys, peer_u, peer_v)))
    depth = w_ada.shape[0]
    bp, lp, d = x_prompt.shape
    bs, ls, _ = x_sample.shape
    assert bs + 1 <= MOD_ROWS
    cond = jnp.zeros((MOD_ROWS, d), F32).at[0].set(c_ctx).at[1:1 + bs].set(c)
    pos = _grid_pos_embed(ls)
    tt_mix = 512
    tt_sel = 512
    sample_tiles_per_batch = ls // tt_mix

    xp, xs = x_prompt, x_sample
    ctx_states = []
    for layer in range(depth):
        q = _prep_params({name: w[layer] for name, w in stacked.items()})
        q["keys_bf"] = q["peer_sub_keys"].astype(BF16)
        q["u_bf"] = q["peer_u"].astype(BF16)
        q["vt_bf"] = _group_transposed(q["peer_v"])
        mod3 = _ada(cond, q["w_ada"], q["b_ada"][None, :]).reshape(MOD_ROWS, 1, 6 * d)

        x1, h2, qq, st = _mixer_group(xp, None, mod3, lambda bi: 0, jnp.zeros((2, bp, D_RNN), F32), q,
                                      tl=lp, bb=32, dt=1024, hy_lanes=1024)
        xp = _peer_group(x1, h2, qq, mod3, lambda ti: 0, q, tt_sel, tt_mix)
        ctx_states.append(jnp.transpose(st, (1, 0, 2)))

        pos_l = pos if layer == 0 else None
        h0 = jnp.transpose(state_rglru[:, layer], (1, 0, 2))
        x1, h2, qq, _ = _mixer_group(xs, pos_l, mod3, lambda bi: bi + 1, h0, q, tl=512, bb=8, dt=512, hy_lanes=256)
        xs = _peer_group(x1, h2, qq, mod3, lambda ti: 1 + ti // sample_tiles_per_batch, q, tt_sel, tt_mix)

    new_state = jnp.stack(ctx_states, axis=1).astype(x_prompt.dtype)
    return (xp, xs, new_state)
```

```python
import functools
import math

import numpy as np
import jax
import jax.numpy as jnp
from jax import lax
from jax.experimental import pallas as pl
from jax.experimental.pallas import tpu as pltpu

F32 = jnp.float32
BF16 = jnp.bfloat16

D_MODEL = 1024
D_RNN = 1024
N_RNN_HEADS = 4
RNN_HEAD_DIM = D_RNN // N_RNN_HEADS
RNN_CONV_W = 4
RGLRU_C = 8.0
D_HY = 1024
HY_ORDER = 2
HY_EMB_BANDS = 16
HY_EMB_DIM = 1 + 2 * HY_EMB_BANDS
HY_EMB_PAD = 64
HY_FILTER_HIDDEN = 64
HY_DECAY_TARGET = 1e-2
HY_MIN_DECAY = math.log(HY_DECAY_TARGET) / 1.5
HY_MAX_DECAY = math.log(HY_DECAY_TARGET) / 0.3
GRID_W = 64
N_KEYS = 128
N_EXPERTS = N_KEYS * N_KEYS
PEER_HEADS = 8
PEER_HALF = 128
PEER_TOPK = 16
DEPTH = 1
DN_ALPHA = (2.0 * DEPTH) ** 0.25
LN_EPS = 1e-5

MOD_ROWS = 16
MIB = 1024 * 1024
NEG_INF = float("-inf")


def _params(semantics, vmem_mib):
    return pltpu.CompilerParams(dimension_semantics=semantics, vmem_limit_bytes=vmem_mib * MIB)


def _gelu(x):
    return jax.nn.gelu(x, approximate=True)


def _dot(a, b):
    return jnp.dot(a, b, preferred_element_type=F32)


def _dot_f32(a, b):
    return jnp.dot(a, b, preferred_element_type=F32, precision=lax.Precision.HIGHEST)


def _ada_body(c_ref, w_ref, b_ref, o_ref):
    c = c_ref[...]
    o_ref[...] = _dot_f32(c * jax.nn.sigmoid(c), w_ref[...]) + b_ref[...]


def _ada(cond, w_ada, b_ada):
    n = w_ada.shape[1]
    tn = 1024
    return pl.pallas_call(
        _ada_body,
        grid=(n // tn,),
        in_specs=[pl.BlockSpec((MOD_ROWS, D_MODEL), lambda j: (0, 0)),
                  pl.BlockSpec((D_MODEL, tn), lambda j: (0, j)),
                  pl.BlockSpec((1, tn), lambda j: (0, j))],
        out_specs=pl.BlockSpec((MOD_ROWS, tn), lambda j: (0, j)),
        out_shape=jax.ShapeDtypeStruct((MOD_ROWS, n), F32),
        compiler_params=_params(("arbitrary",), 32),
        name="ada",
    )(cond, w_ada, b_ada)


def _inproj_body(*refs, has_pos):
    if has_pos:
        x_ref, pos_ref, mod_ref, w_ref, rx_ref, gg_ref, hy_ref, ga_ref, gb_ref = refs
    else:
        x_ref, mod_ref, w_ref, rx_ref, gg_ref, hy_ref, ga_ref, gb_ref = refs
    x = x_ref[0]
    if has_pos:
        x = x + pos_ref[...]
    mod = mod_ref[0]
    sh1 = mod[:, 0:D_MODEL]
    sc1 = mod[:, D_MODEL:2 * D_MODEL]
    h = (x * (1.0 + sc1) + sh1).astype(BF16)
    o = 0
    rx_ref[...] = _dot(h, w_ref[:, o:o + D_RNN])
    o += D_RNN
    gg_ref[0] = _gelu(_dot(h, w_ref[:, o:o + D_RNN])).astype(BF16)
    o += D_RNN
    for j in range(3):
        hy_ref[0, :, j * D_HY:(j + 1) * D_HY] = _dot(h, w_ref[:, o:o + D_HY])
        o += D_HY
    ga_ref[0] = jax.nn.sigmoid(_dot(h, w_ref[:, o:o + D_MODEL])).astype(BF16)
    o += D_MODEL
    gb_ref[0] = jax.nn.sigmoid(_dot(h, w_ref[:, o:o + D_MODEL])).astype(BF16)


def _inproj(x, pos, mod3, w_in_bf, mod_row_of_batch, tl):
    b, l, _ = x.shape
    d_in = w_in_bf.shape[1]
    has_pos = pos is not None
    in_specs = [pl.BlockSpec((1, tl, D_MODEL), lambda bi, ti: (bi, ti, 0))]
    args = [x]
    if has_pos:
        in_specs.append(pl.BlockSpec((tl, D_MODEL), lambda bi, ti: (ti, 0)))
        args.append(pos)
    in_specs += [pl.BlockSpec((1, 1, 6 * D_MODEL), lambda bi, ti: (mod_row_of_batch(bi), 0, 0)),
                 pl.BlockSpec((D_MODEL, d_in), lambda bi, ti: (0, 0), pipeline_mode=pl.Buffered(1))]
    args += [mod3, w_in_bf]
    tok = lambda bi, ti: (bi, ti, 0)
    out_specs = [pl.BlockSpec((tl, D_RNN), lambda bi, ti: (ti, bi)),
                 pl.BlockSpec((1, tl, D_RNN), tok),
                 pl.BlockSpec((1, tl, 3 * D_HY), tok),
                 pl.BlockSpec((1, tl, D_MODEL), tok),
                 pl.BlockSpec((1, tl, D_MODEL), tok)]
    out_shape = [jax.ShapeDtypeStruct((l, b * D_RNN), F32),
                 jax.ShapeDtypeStruct((b, l, D_RNN), BF16),
                 jax.ShapeDtypeStruct((b, l, 3 * D_HY), F32),
                 jax.ShapeDtypeStruct((b, l, D_MODEL), BF16),
                 jax.ShapeDtypeStruct((b, l, D_MODEL), BF16)]
    return pl.pallas_call(
        functools.partial(_inproj_body, has_pos=has_pos),
        grid=(b, l // tl),
        in_specs=in_specs, out_specs=out_specs, out_shape=out_shape,
        compiler_params=_params(("parallel", "parallel"), 56),
        name="inproj",
    )(*args)


def _softplus(x):
    return jnp.maximum(x, 0.0) + jnp.log(1.0 + jnp.exp(-jnp.abs(x)))


def _rglru_body(rx_ref, h0_ref, wg_ref, bg_ref, lam_ref, cw_ref, cb_ref, y_ref, st_ref, a_s, u_s,
                *, seq, bb, tc):
    hd = RNN_HEAD_DIM
    nchunks = seq // tc
    cw = cw_ref[...]
    cb = cb_ref[...]
    for d in range(2):
        coef = -RGLRU_C * _softplus(-lam_ref[0, d:d + 1, :])
        wd = wg_ref[0, :, 2 * d * hd:2 * (d + 1) * hd]
        bd = bg_ref[0, :, 2 * d * hd:2 * (d + 1) * hd]

        def chunk(ci, h, d=d, coef=coef, wd=wd, bd=bd):
            c = ci if d == 0 else nchunks - 1 - ci
            t0 = pl.multiple_of(c * tc, tc)
            lo_ok = jnp.where(c > 0, 1.0, 0.0)
            hi_ok = jnp.where(c < nchunks - 1, 1.0, 0.0)
            lo = rx_ref[pl.ds(jnp.maximum(t0 - 2, 0), 2)] * lo_ok
            hi = rx_ref[pl.ds(jnp.minimum(t0 + tc, seq - 1), 1)] * hi_ok
            xe = jnp.concatenate([lo, rx_ref[pl.ds(t0, tc)], hi], axis=0)
            xc = cb[None] + sum(cw[k:k + 1][None] * xe[k:k + tc] for k in range(RNN_CONV_W))
            xc2 = xc.reshape(tc * bb, hd)
            g = _dot(xc2.astype(BF16), wd) + bd
            r = jax.nn.sigmoid(g[:, :hd])
            i = jax.nn.sigmoid(g[:, hd:])
            a = jnp.exp(coef * r)
            u = jnp.sqrt(1.0 - a * a) * (i * xc2)
            a_s[...] = a.reshape(tc, bb, hd)
            u_s[...] = u.reshape(tc, bb, hd)

            def step(j, h):
                tau = j if d == 0 else tc - 1 - j
                h = a_s[tau] * h + u_s[tau]
                if d == 0:
                    y_ref[t0 + tau] = h
                else:
                    y_ref[t0 + tau] = y_ref[t0 + tau] + h
                return h

            return lax.fori_loop(0, tc, step, h, unroll=8)

        h_fin = lax.fori_loop(0, nchunks, chunk, h0_ref[d])
        st_ref[d] = h_fin


def _rglru(rx3, h0, wg, bg, lam, cw, cb, bb):
    seq, b, _ = rx3.shape
    hd = RNN_HEAD_DIM
    tc = max(8, 512 // bb)
    blk = lambda bi, hi: (0, bi, hi)
    return pl.pallas_call(
        functools.partial(_rglru_body, seq=seq, bb=bb, tc=tc),
        grid=(b // bb, N_RNN_HEADS),
        in_specs=[pl.BlockSpec((seq, bb, hd), blk),
                  pl.BlockSpec((2, bb, hd), blk),
                  pl.BlockSpec((1, hd, 4 * hd), lambda bi, hi: (hi, 0, 0)),
                  pl.BlockSpec((1, 1, 4 * hd), lambda bi, hi: (hi, 0, 0)),
                  pl.BlockSpec((1, 2, hd), lambda bi, hi: (hi, 0, 0)),
                  pl.BlockSpec((RNN_CONV_W, hd), lambda bi, hi: (0, hi)),
                  pl.BlockSpec((1, hd), lambda bi, hi: (0, hi))],
        out_specs=[pl.BlockSpec((seq, bb, hd), blk),
                   pl.BlockSpec((2, bb, hd), blk)],
        out_shape=[jax.ShapeDtypeStruct((seq, b, D_RNN), F32),
                   jax.ShapeDtypeStruct((2, b, D_RNN), F32)],
        scratch_shapes=[pltpu.VMEM((tc, bb, hd), F32), pltpu.VMEM((tc, bb, hd), F32)],
        compiler_params=_params(("parallel", "parallel"), 56),
        name="rglru",
    )(rx3, h0, wg, bg, lam, cw, cb)


def _dft_matrices(seq):
    k = np.arange(seq, dtype=np.float64)[:, None]
    s = np.arange(seq, dtype=np.float64)[None, :]
    ang = np.pi * k * s / seq
    top = np.cos(ang)
    bot = -np.sin(ang)
    bot[0, :] = np.where(np.arange(seq) % 2 == 0, 1.0, -1.0)
    fwd = np.concatenate([top, bot], axis=0)
    return jnp.asarray(fwd, dtype=BF16), jnp.asarray(fwd.T, dtype=BF16)


def _filter_features(seq):
    t = np.arange(seq, dtype=np.float32)
    t_norm = t / np.float32(max(seq - 1, 1))
    w = (np.float32(2.0 * math.pi) * t / np.float32(seq)).astype(np.float32)
    bands = np.linspace(1e-4, HY_EMB_BANDS - 1, HY_EMB_BANDS, dtype=np.float32)
    fw = w[:, None] * bands[None, :]
    z = np.concatenate([t_norm[:, None], np.cos(fw), -np.sin(fw)], axis=-1).astype(np.float32)
    zp = np.zeros((seq, HY_EMB_PAD), np.float32)
    zp[:, :HY_EMB_DIM] = z
    deltas = np.abs(np.linspace(HY_MIN_DECAY, HY_MAX_DECAY, D_HY, dtype=np.float32))
    return jnp.asarray(zp), jnp.asarray(t_norm[:, None]), jnp.asarray(deltas[None, :])


def _dot_split(f_bf, x):
    hi = x.astype(BF16)
    lo = (x - hi.astype(F32)).astype(BF16)
    return _dot(f_bf, hi) + _dot(f_bf, lo)


def _hyfilt_body(z_ref, tn_ref, dl_ref, w1_ref, b1_ref, w2_ref, b2_ref, fr_ref, w3f_ref, b3f_ref,
                 w3b_ref, b3b_ref, f_ref, kr_ref, ki_ref, *, seq):
    freq = fr_ref[...]
    hid = jnp.sin(freq * (_dot_f32(z_ref[...], w1_ref[...]) + b1_ref[...]))
    hid = jnp.sin(freq * (_dot_f32(hid, w2_ref[...]) + b2_ref[...]))
    decay = jnp.exp(-tn_ref[...] * dl_ref[...])
    ff = (_dot_f32(hid, w3f_ref[...]) + b3f_ref[...]) * decay
    fb = (_dot_f32(hid, w3b_ref[...]) + b3b_ref[...]) * decay
    row = lax.broadcasted_iota(jnp.int32, ff.shape, 0)
    fb = jnp.where(row == 0, 0.0, fb)
    s = ff + fb
    dm = ff - fb
    sign = jnp.where((row & 1) == 0, 1.0, -1.0)
    nyq = jnp.sum(s * sign, axis=0, keepdims=True)
    kr = _dot_split(f_ref[0:seq, :], s)
    ki = _dot_split(f_ref[seq:2 * seq, :], dm)
    scale = jnp.where(row == 0, 0.5 / seq, 1.0 / seq)
    kr_ref[0] = kr * scale
    ki_ref[0] = jnp.where(row == 0, nyq, ki) * scale


def _hyfilt(seq, fwd, p):
    z, tn, dl = _filter_features(seq)
    dt = 512
    nd = D_HY // dt
    hidn = HY_FILTER_HIDDEN
    const = lambda o, j: (0, 0)
    w3 = p["hy_ffn_w3"]
    b3 = p["hy_ffn_b3"][None, :]
    in_specs = [pl.BlockSpec((seq, HY_EMB_PAD), const),
                pl.BlockSpec((seq, 1), const),
                pl.BlockSpec((1, dt), lambda o, j: (0, j)),
                pl.BlockSpec((HY_EMB_PAD, hidn), const),
                pl.BlockSpec((1, hidn), const),
                pl.BlockSpec((hidn, hidn), const),
                pl.BlockSpec((1, hidn), const),
                pl.BlockSpec((1, hidn), const),
                pl.BlockSpec((hidn, dt), lambda o, j: (0, o * nd + j)),
                pl.BlockSpec((1, dt), lambda o, j: (0, o * nd + j)),
                pl.BlockSpec((hidn, dt), lambda o, j: (0, (HY_ORDER + o) * nd + j)),
                pl.BlockSpec((1, dt), lambda o, j: (0, (HY_ORDER + o) * nd + j)),
                pl.BlockSpec((2 * seq, seq), const)]
    w1p = jnp.zeros((HY_EMB_PAD, hidn), F32).at[:HY_EMB_DIM].set(p["hy_ffn_w1"])
    out_spec = pl.BlockSpec((1, seq, dt), lambda o, j: (o, 0, j))
    return pl.pallas_call(
        functools.partial(_hyfilt_body, seq=seq),
        grid=(HY_ORDER, nd),
        in_specs=in_specs,
        out_specs=[out_spec, out_spec],
        out_shape=[jax.ShapeDtypeStruct((HY_ORDER, seq, D_HY), F32)] * 2,
        compiler_params=_params(("parallel", "parallel"), 48),
        name="hyfilt",
    )(z, tn, dl, w1p, p["hy_ffn_b1"][None, :], p["hy_ffn_w2"], p["hy_ffn_b2"][None, :],
      p["hy_sin_freq"][None, :], w3, b3, w3, b3, fwd)


def _hyena_body(x1_ref, x2_ref, v_ref, w1_ref, w2_ref, wv_ref, b1_ref, b2_ref, bv_ref,
                kr_ref, ki_ref, sk_ref, f_ref, ft_ref, o_ref, *, seq, lanes):
    dt = v_ref.shape[2]
    row = lax.broadcasted_iota(jnp.int32, (seq, lanes), 0)
    first = row == 0
    last = row == seq - 1

    def conv3(x_ref, w_ref, b_ref, cols):
        x = x_ref[0, :, cols]
        xm = jnp.where(first, 0.0, pltpu.roll(x, 1, 0))
        xp = jnp.where(last, 0.0, pltpu.roll(x, seq - 1, 0))
        return w_ref[0:1, cols] * xm + w_ref[1:2, cols] * x + w_ref[2:3, cols] * xp + b_ref[:, cols]

    def spectrum_product(uf, o, cols):
        kr = kr_ref[o, :, cols]
        ki = ki_ref[o, :, cols]
        kiz = jnp.where(first, 0.0, ki)
        krb = jnp.where(first, ki, kr)
        top = uf[:seq]
        bot = uf[seq:]
        return jnp.concatenate([top * kr - bot * kiz, top * kiz + bot * krb], axis=0).astype(BF16)

    chains = [slice(c * lanes, (c + 1) * lanes) for c in range(dt // lanes)]
    u = [conv3(v_ref, wv_ref, bv_ref, cols) for cols in chains]
    for o, gate_ref, gw_ref, gb_ref in ((0, x1_ref, w1_ref, b1_ref), (1, x2_ref, w2_ref, b2_ref)):
        uf = [_dot(f_ref[...], uc.astype(BF16)) for uc in u]
        y = []
        for cols, ufc in zip(chains, uf):
            y.append(_dot(ft_ref[...], spectrum_product(ufc, o, cols)))
        u = [conv3(gate_ref, gw_ref, gb_ref, cols) * (yc + uc * sk_ref[o:o + 1, cols])
             for cols, yc, uc in zip(chains, y, u)]
    for cols, z in zip(chains, u):
        o_ref[0, :, cols] = z.astype(BF16)


def _hyena(hy, cw, cb, kr, ki, skip, fwd, inv, dt, lanes):
    b, seq, _ = hy.shape
    nd = D_HY // dt
    const = lambda j, bi: (0, 0)
    part = lambda k: (lambda j, bi: (bi, 0, k * nd + j))
    wpart = lambda k: (lambda j, bi: (0, k * nd + j))
    once = pl.Buffered(1)
    in_specs = ([pl.BlockSpec((1, seq, dt), part(k)) for k in range(3)]
                + [pl.BlockSpec((3, dt), wpart(k)) for k in range(3)]
                + [pl.BlockSpec((1, dt), wpart(k)) for k in range(3)]
                + [pl.BlockSpec((HY_ORDER, seq, dt), lambda j, bi: (0, 0, j), pipeline_mode=once),
                   pl.BlockSpec((HY_ORDER, seq, dt), lambda j, bi: (0, 0, j), pipeline_mode=once),
                   pl.BlockSpec((HY_ORDER, dt), lambda j, bi: (0, j)),
                   pl.BlockSpec((2 * seq, seq), const, pipeline_mode=once),
                   pl.BlockSpec((seq, 2 * seq), const, pipeline_mode=once)])
    return pl.pallas_call(
        functools.partial(_hyena_body, seq=seq, lanes=min(dt, lanes)),
        grid=(nd, b),
        in_specs=in_specs,
        out_specs=pl.BlockSpec((1, seq, dt), lambda j, bi: (bi, 0, j)),
        out_shape=jax.ShapeDtypeStruct((b, seq, D_HY), BF16),
        compiler_params=_params(("parallel", "parallel"), 56),
        name="hyena",
    )(hy, hy, hy, cw, cw, cw, cb, cb, cb, kr, ki, skip, fwd, inv)


def _layer_norm(x, g, b):
    mu = jnp.mean(x, axis=-1, keepdims=True)
    xc = x - mu
    var = jnp.mean(xc * xc, axis=-1, keepdims=True)
    return xc * lax.rsqrt(var + LN_EPS) * g + b


def _merge_body(*refs, has_pos):
    if has_pos:
        (x_ref, pos_ref, mod_ref, yr_ref, gg_ref, z_ref, ga_ref, gb_ref, wa_ref, wb_ref, wo_ref,
         lg_ref, lb_ref, wq_ref, x1_ref, h2_ref, q_ref) = refs
    else:
        (x_ref, mod_ref, yr_ref, gg_ref, z_ref, ga_ref, gb_ref, wa_ref, wb_ref, wo_ref,
         lg_ref, lb_ref, wq_ref, x1_ref, h2_ref, q_ref) = refs
    d = D_MODEL
    mod = mod_ref[0]
    g1 = mod[:, 2 * d:3 * d]
    sh2 = mod[:, 3 * d:4 * d]
    sc2 = mod[:, 4 * d:5 * d]
    ya = _dot((yr_ref[...].astype(BF16) * gg_ref[0]), wa_ref[...])
    yb = _dot(z_ref[0], wb_ref[...])
    m = ga_ref[0].astype(F32) * ya + gb_ref[0].astype(F32) * yb
    y = _dot(m.astype(BF16), wo_ref[...])
    x = x_ref[0]
    if has_pos:
        x = x + pos_ref[...]
    x1 = _layer_norm(DN_ALPHA * x + g1 * y, lg_ref[...], lb_ref[...])
    x1_ref[0] = x1
    h2 = (x1 * (1.0 + sc2) + sh2).astype(BF16)
    h2_ref[0] = h2
    q_ref[0] = _dot(h2, wq_ref[...]).astype(BF16)


def _merge(x, pos, mod3, mod_row_of_batch, yr2, gg, z, ga, gb, wa, wb, wo, lg, lb, wq, tl):
    b, l, d = x.shape
    nq = wq.shape[1]
    has_pos = pos is not None
    tok = lambda bi, ti: (bi, ti, 0)
    const = lambda bi, ti: (0, 0)
    in_specs = [pl.BlockSpec((1, tl, d), tok)]
    args = [x]
    if has_pos:
        in_specs.append(pl.BlockSpec((tl, d), lambda bi, ti: (ti, 0)))
        args.append(pos)
    in_specs += [pl.BlockSpec((1, 1, 6 * d), lambda bi, ti: (mod_row_of_batch(bi), 0, 0)),
                 pl.BlockSpec((tl, D_RNN), lambda bi, ti: (ti, bi)),
                 pl.BlockSpec((1, tl, d), tok), pl.BlockSpec((1, tl, d), tok),
                 pl.BlockSpec((1, tl, d), tok), pl.BlockSpec((1, tl, d), tok),
                 pl.BlockSpec((d, d), const), pl.BlockSpec((d, d), const), pl.BlockSpec((d, d), const),
                 pl.BlockSpec((1, d), const), pl.BlockSpec((1, d), const),
                 pl.BlockSpec((d, nq), const)]
    args += [mod3, yr2, gg, z, ga, gb, wa, wb, wo, lg, lb, wq]
    return pl.pallas_call(
        functools.partial(_merge_body, has_pos=has_pos),
        grid=(b, l // tl),
        in_specs=in_specs,
        out_specs=[pl.BlockSpec((1, tl, d), tok), pl.BlockSpec((1, tl, d), tok),
                   pl.BlockSpec((1, tl, nq), tok)],
        out_shape=[jax.ShapeDtypeStruct((b, l, d), F32), jax.ShapeDtypeStruct((b, l, d), BF16),
                   jax.ShapeDtypeStruct((b, l, nq), BF16)],
        compiler_params=_params(("parallel", "parallel"), 56),
        name="merge",
    )(*args)


RANK_NONE = 1.0e9


def _topk_rank(s, k, exact, want_rank=True):
    n = s.shape[0]
    s0 = s
    rank = jnp.full(s.shape, RANK_NONE, F32) if want_rank else None
    vals = []
    if exact:
        iota = lax.broadcasted_iota(jnp.int32, s.shape, 0).astype(F32)
    for r in range(k):
        m = jnp.max(s, axis=0, keepdims=True)
        hit = s == m
        if exact:
            idx = jnp.min(jnp.where(hit, iota, float(n)), axis=0, keepdims=True)
            hit = iota == idx
        if want_rank:
            rank = jnp.where(hit, float(r), rank)
        s = jnp.where(hit, NEG_INF, s)
        vals.append(m)
    taken = jnp.where(s != s0, 1.0, 0.0)
    if exact:
        ties = jnp.zeros((1, s.shape[1]), F32)
    else:
        ties = jnp.where(jnp.sum(taken, axis=0, keepdims=True) > float(k), 1.0, 0.0)
    return vals, rank, taken, ties


def _select_tile(s1, s2, exact):
    k = PEER_TOPK
    v1, rank1, _, t1 = _topk_rank(s1, k, exact, want_rank=exact)
    v2, rank2, _, t2 = _topk_rank(s2, k, exact)
    v2_16 = jnp.concatenate(v2, axis=0)
    v2_8 = v2_16[0:8]
    v1_hi = jnp.concatenate(v1[8:16], axis=0)
    cand = jnp.concatenate([v1[0] + v2_16] + [v1[a] + v2_8 for a in range(1, 8)] + [v1_hi + v2[0]], axis=0)
    _, _, sel, tc = _topk_rank(cand, k, exact, want_rank=False)
    z = jnp.sum(sel * jnp.exp(cand - cand[0:1]), axis=0, keepdims=True)
    counts = [jnp.sum(sel[0:16], axis=0, keepdims=True)]
    counts += [jnp.sum(sel[8 + 8 * a:16 + 8 * a], axis=0, keepdims=True) for a in range(1, 8)]
    counts += [sel[72 + a:73 + a] for a in range(8)]
    ni = jnp.zeros(s1.shape, F32)
    for a in range(k):
        is_a = (rank1 == float(a)) if exact else (s1 == v1[a])
        ni = jnp.where(is_a, counts[a], ni)
    e1n = jnp.exp(s1 - v1[0]) / z
    e2 = jnp.exp(s2 - v2[0])
    return e1n, ni, e2.astype(BF16), rank2.astype(BF16), t1 + t2 + tc


def _peersel_body(q_ref, k_ref, e1n_ref, ni_ref, e2_ref, r2_ref, s1_ref, s2_ref):
    nt = (((1,), (1,)), ((), ()))
    s1_ref[...] = lax.dot_general(k_ref[0], q_ref[:, 0:PEER_HALF], nt, preferred_element_type=F32)
    s2_ref[...] = lax.dot_general(k_ref[1], q_ref[:, PEER_HALF:2 * PEER_HALF], nt, preferred_element_type=F32)
    lane = 128
    n_tiles = s1_ref.shape[1] // lane

    def run(cols, exact):
        e1n, ni, e2, r2, ties = _select_tile(s1_ref[:, cols], s2_ref[:, cols], exact)
        e1n_ref[0, :, cols] = e1n
        ni_ref[0, :, cols] = ni
        e2_ref[0, :, cols] = e2
        r2_ref[0, :, cols] = r2
        return ties

    tiles = [slice(lt * lane, (lt + 1) * lane) for lt in range(n_tiles)]
    ties = [run(cols, exact=False) for cols in tiles]
    for cols, t in zip(tiles, ties):
        @pl.when(jnp.max(t) > 0.0)
        def _(cols=cols):
            run(cols, exact=True)


def _peersel(q2, keys_bf, tt):
    t = q2.shape[0]
    out_spec = pl.BlockSpec((1, N_KEYS, tt), lambda ti, h: (h, 0, ti))
    return pl.pallas_call(
        _peersel_body,
        grid=(t // tt, PEER_HEADS),
        in_specs=[pl.BlockSpec((tt, 2 * PEER_HALF), lambda ti, h: (ti, h)),
                  pl.BlockSpec((2, N_KEYS, PEER_HALF), lambda ti, h: (0, 0, 0))],
        out_specs=[out_spec] * 4,
        out_shape=[jax.ShapeDtypeStruct((PEER_HEADS, N_KEYS, t), F32)] * 2
        + [jax.ShapeDtypeStruct((PEER_HEADS, N_KEYS, t), BF16)] * 2,
        scratch_shapes=[pltpu.VMEM((N_KEYS, tt), F32), pltpu.VMEM((N_KEYS, tt), F32)],
        compiler_params=_params(("parallel", "parallel"), 48),
        name="peersel",
    )(q2, keys_bf)


PEER_EB = 2048
PEER_EG = 512


def _gelu_sigmoid(x):
    k1 = 2.0 * math.sqrt(2.0 / math.pi)
    k2 = k1 * 0.044715
    z = x * (-k1 - k2 * (x * x))
    return x * (1.0 / (1.0 + jnp.exp(z)))


def _peermix_body(h2_ref, u_ref, vt_ref, e1n_ref, ni_ref, e2_ref, r2_ref, x1_ref, mod_ref, lg_ref, lb_ref,
                  o_ref, acc_ref, w_ref, act_ref, g_ref, h2t_ref):
    e = pl.program_id(1)
    tt = acc_ref.shape[1]
    n_groups = PEER_EB // PEER_EG
    slabs_per_group = PEER_EG // N_KEYS
    pack = 16
    n_chunks = N_KEYS // pack
    lanes = 256
    zero = jnp.zeros((), BF16)

    @pl.when(e == 0)
    def _():
        acc_ref[...] = jnp.zeros(acc_ref.shape, F32)
        h2t_ref[...] = h2_ref[...].T

    def selection_weights(s):
        for lh in range(tt // lanes):
            cols = slice(lh * lanes, (lh + 1) * lanes)
            g = [None] * n_chunks
            for h in range(PEER_HEADS):
                nrow = jnp.broadcast_to(ni_ref[h, s:s + 1, cols], (pack, lanes)).astype(BF16)
                erow = jnp.broadcast_to(e1n_ref[h, s:s + 1, cols], (pack, lanes)).astype(BF16)
                for c in range(n_chunks):
                    rows = slice(c * pack, (c + 1) * pack)
                    term = jnp.where(r2_ref[h, rows, cols] < nrow, e2_ref[h, rows, cols], zero) * erow
                    g[c] = term if g[c] is None else g[c] + term
            for c in range(n_chunks):
                g_ref[s * N_KEYS + c * pack:s * N_KEYS + (c + 1) * pack, cols] = g[c]

    for p in range(n_groups):
        grp = slice(p * PEER_EG, (p + 1) * PEER_EG)
        act_ref[grp, :] = _dot(u_ref[grp, :], h2t_ref[...])
        for s in range(p * slabs_per_group, (p + 1) * slabs_per_group):
            selection_weights(s)
    for p in range(n_groups):
        grp = slice(p * PEER_EG, (p + 1) * PEER_EG)
        for c in range(PEER_EG // pack):
            rows = slice(p * PEER_EG + c * pack, p * PEER_EG + (c + 1) * pack)
            w_ref[rows, :] = _gelu_sigmoid(act_ref[rows, :]).astype(BF16) * g_ref[rows, :]
        acc_ref[...] += _dot(vt_ref[:, grp], w_ref[grp, :])

    @pl.when(e == pl.num_programs(1) - 1)
    def _():
        d = D_MODEL
        g2 = mod_ref[0][:, 5 * d:6 * d]
        y = acc_ref[...].T
        o_ref[...] = _layer_norm(DN_ALPHA * x1_ref[...] + g2 * y, lg_ref[...], lb_ref[...])


def _peermix(h2, u_bf, vt_bf, e1n, ni, e2, r2, x1, mod3, mod_row_of_tile, lg, lb, tt):
    t, d = h2.shape
    nb = u_bf.shape[0] // PEER_EB
    ns = PEER_EB // N_KEYS
    tile = lambda ti, e: (ti, 0)
    const = lambda ti, e: (0, 0)
    return pl.pallas_call(
        _peermix_body,
        grid=(t // tt, nb),
        in_specs=[pl.BlockSpec((tt, d), tile),
                  pl.BlockSpec((PEER_EB, d), lambda ti, e: (e, 0)),
                  pl.BlockSpec((d, PEER_EB), lambda ti, e: (0, e)),
                  pl.BlockSpec((PEER_HEADS, ns, tt), lambda ti, e: (0, e, ti)),
                  pl.BlockSpec((PEER_HEADS, ns, tt), lambda ti, e: (0, e, ti)),
                  pl.BlockSpec((PEER_HEADS, N_KEYS, tt), lambda ti, e: (0, 0, ti)),
                  pl.BlockSpec((PEER_HEADS, N_KEYS, tt), lambda ti, e: (0, 0, ti)),
                  pl.BlockSpec((tt, d), tile),
                  pl.BlockSpec((1, 1, 6 * d), lambda ti, e: (mod_row_of_tile(ti), 0, 0)),
                  pl.BlockSpec((1, d), const), pl.BlockSpec((1, d), const)],
        out_specs=pl.BlockSpec((tt, d), tile),
        out_shape=jax.ShapeDtypeStruct((t, d), F32),
        scratch_shapes=[pltpu.VMEM((d, tt), F32), pltpu.VMEM((PEER_EB, tt), BF16),
                        pltpu.VMEM((PEER_EB, tt), F32), pltpu.VMEM((PEER_EB, tt), BF16),
                        pltpu.VMEM((d, tt), BF16)],
        compiler_params=_params(("parallel", "arbitrary"), 56),
        name="peermix",
    )(h2, u_bf, vt_bf, e1n, ni, e2, r2, x1, mod3, lg, lb)


def _prep_params(p):
    hd = RNN_HEAD_DIM
    q = dict(p)
    q["w_in_bf"] = p["w_in"].astype(BF16)
    q["wg"] = jnp.transpose(p["rnn_gate_w"], (2, 3, 0, 1, 4)).reshape(N_RNN_HEADS, hd, 4 * hd).astype(BF16)
    q["bg"] = jnp.transpose(p["rnn_gate_b"].reshape(2, 2, N_RNN_HEADS, hd), (2, 0, 1, 3)).reshape(
        N_RNN_HEADS, 1, 4 * hd)
    q["lam"] = jnp.transpose(p["rnn_lambda"].reshape(2, N_RNN_HEADS, hd), (1, 0, 2))
    q["rnn_cb"] = p["rnn_conv_b"][None, :]
    q["hy_cb"] = p["hy_conv_b"][None, :]
    q["wa"] = p["w_branch_a"].astype(BF16)
    q["wb"] = p["w_branch_b"].astype(BF16)
    q["wo"] = p["w_out"].astype(BF16)
    q["wq"] = p["peer_w_query"].astype(BF16)
    q["ln1_g2"] = p["ln1_g"][None, :]
    q["ln1_b2"] = p["ln1_b"][None, :]
    q["ln2_g2"] = p["ln2_g"][None, :]
    q["ln2_b2"] = p["ln2_b"][None, :]
    return q


def _mixer_group(x, pos, mod3, mod_row_of_batch, h0, q, tl, bb, dt, hy_lanes):
    b, l, _ = x.shape
    fwd, inv = _dft_matrices(l)
    kr, ki = _hyfilt(l, fwd, q)
    rx2, gg, hy, ga, gb = _inproj(x, pos, mod3, q["w_in_bf"], mod_row_of_batch, tl)
    y3, st = _rglru(rx2.reshape(l, b, D_RNN), h0, q["wg"], q["bg"], q["lam"], q["rnn_conv_w"],
                    q["rnn_cb"], bb)
    z = _hyena(hy, q["hy_conv_w"], q["hy_cb"], kr, ki, q["hy_skip"], fwd, inv, dt, hy_lanes)
    x1, h2, qq = _merge(x, pos, mod3, mod_row_of_batch, y3.reshape(l, b * D_RNN), gg, z, ga, gb,
                        q["wa"], q["wb"], q["wo"], q["ln1_g2"], q["ln1_b2"], q["wq"], tl)
    return x1, h2, qq, st


def _group_transposed(v):
    return v.astype(BF16).T


def _peer_group(x1, h2, qq, mod3, mod_row_of_tile, q, tt_sel, tt_mix):
    b, l, d = x1.shape
    t = b * l
    e1n, ni, e2, r2 = _peersel(qq.reshape(t, qq.shape[-1]), q["keys_bf"], tt_sel)
    out = _peermix(h2.reshape(t, d), q["u_bf"], q["vt_bf"], e1n, ni, e2, r2, x1.reshape(t, d), mod3,
                   mod_row_of_tile, q["ln2_g2"], q["ln2_b2"], tt_mix)
    return out.reshape(b, l, d)


def _grid_pos_embed(n_tokens):
    rows = n_tokens // GRID_W
    t = np.arange(rows * GRID_W)
    r = (t // GRID_W).astype(np.float32)
    col = (t % GRID_W).astype(np.float32)
    quarter = D_MODEL // 4
    omega = (1.0 / (10000.0 ** (np.arange(quarter, dtype=np.float32) / np.float32(quarter)))).astype(np.float32)
    er = r[:, None] * omega[None, :]
    ec = col[:, None] * omega[None, :]
    return jnp.asarray(np.concatenate([np.sin(er), np.cos(er), np.sin(ec), np.cos(ec)], axis=-1), dtype=F32)


PARAM_NAMES = ("w_ada", "b_ada", "w_in", "rnn_conv_w", "rnn_conv_b", "rnn_gate_w", "rnn_gate_b", "rnn_lambda",
               "hy_conv_w", "hy_conv_b", "hy_ffn_w1", "hy_ffn_b1", "hy_ffn_w2", "hy_ffn_b2", "hy_ffn_w3",
               "hy_ffn_b3", "hy_sin_freq", "hy_skip", "w_branch_a", "w_branch_b", "w_out", "ln1_g", "ln1_b",
               "ln2_g", "ln2_b", "peer_w_query", "peer_sub_keys", "peer_u", "peer_v")


def kernel(x_prompt, x_sample, state_rglru, c, c_ctx, w_ada, b_ada, w_in, rnn_conv_w, rnn_conv_b, rnn_gate_w,
           rnn_gate_b, rnn_lambda, hy_conv_w, hy_conv_b, hy_ffn_w1, hy_ffn_b1, hy_ffn_w2, hy_ffn_b2, hy_ffn_w3,
           hy_ffn_b3, hy_sin_freq, hy_skip, w_branch_a, w_branch_b, w_out, ln1_g, ln1_b, ln2_g, ln2_b,
           peer_w_query, peer_sub_keys, peer_u, peer_v):
    stacked = dict(zip(PARAM_NAMES, (w_ada, b_ada, w_in, rnn_conv_w, rnn_conv_b, rnn_gate_w, rnn_gate_b,
                                     rnn_lambda, hy_conv_w, hy_conv_b, hy_ffn_w1, hy_ffn_b1, hy_ffn_w2,
                                     hy_ffn_b2, hy_ffn_w3, hy_ffn_b3, hy_sin_freq, hy_skip, w_branch_a,
                                     w_branch_b, w_out, ln1_g, ln1_b, ln2_g, ln2_b, peer_w_query,
                                     peer_sub_keys, peer_u, peer_v)))
    depth = w_ada.shape[0]
    bp, lp, d = x_prompt.shape
    bs, ls, _ = x_sample.shape
    assert bs + 1 <= MOD_ROWS
    cond = jnp.zeros((MOD_ROWS, d), F32).at[0].set(c_ctx).at[1:1 + bs].set(c)
    pos = _grid_pos_embed(ls)
    tt_mix = 512
    tt_sel = 512
    sample_tiles_per_batch = ls // tt_mix

    xp, xs = x_prompt, x_sample
    ctx_states = []
    for layer in range(depth):
        q = _prep_params({name: w[layer] for name, w in stacked.items()})
        q["keys_bf"] = q["peer_sub_keys"].astype(BF16)
        q["u_bf"] = q["peer_u"].astype(BF16)
        q["vt_bf"] = _group_transposed(q["peer_v"])
        mod3 = _ada(cond, q["w_ada"], q["b_ada"][None, :]).reshape(MOD_ROWS, 1, 6 * d)

        x1, h2, qq, st = _mixer_group(xp, None, mod3, lambda bi: 0, jnp.zeros((2, bp, D_RNN), F32), q,
                                      tl=lp, bb=32, dt=1024, hy_lanes=512)
        xp = _peer_group(x1, h2, qq, mod3, lambda ti: 0, q, tt_sel, tt_mix)
        ctx_states.append(jnp.transpose(st, (1, 0, 2)))

        pos_l = pos if layer == 0 else None
        h0 = jnp.transpose(state_rglru[:, layer], (1, 0, 2))
        x1, h2, qq, _ = _mixer_group(xs, pos_l, mod3, lambda bi: bi + 1, h0, q, tl=512, bb=8, dt=512, hy_lanes=256)
        xs = _peer_group(x1, h2, qq, mod3, lambda ti: 1 + ti // sample_tiles_per_batch, q, tt_sel, tt_mix)

    new_state = jnp.stack(ctx_states, axis=1).astype(x_prompt.dtype)
    return (xp, xs, new_state)
```

```python
import functools
import math

import numpy as np
import jax
import jax.numpy as jnp
from jax import lax
from jax.experimental import pallas as pl
from jax.experimental.pallas import tpu as pltpu

F32 = jnp.float32
BF16 = jnp.bfloat16

D_MODEL = 1024
D_RNN = 1024
N_RNN_HEADS = 4
RNN_HEAD_DIM = D_RNN // N_RNN_HEADS
RNN_CONV_W = 4
RGLRU_C = 8.0
D_HY = 1024
HY_ORDER = 2
HY_EMB_BANDS = 16
HY_EMB_DIM = 1 + 2 * HY_EMB_BANDS
HY_EMB_PAD = 64
HY_FILTER_HIDDEN = 64
HY_DECAY_TARGET = 1e-2
HY_MIN_DECAY = math.log(HY_DECAY_TARGET) / 1.5
HY_MAX_DECAY = math.log(HY_DECAY_TARGET) / 0.3
GRID_W = 64
N_KEYS = 128
N_EXPERTS = N_KEYS * N_KEYS
PEER_HEADS = 8
PEER_HALF = 128
PEER_TOPK = 16
DEPTH = 1
DN_ALPHA = (2.0 * DEPTH) ** 0.25
LN_EPS = 1e-5

MOD_ROWS = 16
MIB = 1024 * 1024
NEG_INF = float("-inf")


def _params(semantics, vmem_mib):
    return pltpu.CompilerParams(dimension_semantics=semantics, vmem_limit_bytes=vmem_mib * MIB)


def _gelu(x):
    return jax.nn.gelu(x, approximate=True)


def _dot(a, b):
    return jnp.dot(a, b, preferred_element_type=F32)


def _dot_f32(a, b):
    return jnp.dot(a, b, preferred_element_type=F32, precision=lax.Precision.HIGHEST)


def _ada_body(c_ref, w_ref, b_ref, o_ref):
    c = c_ref[...]
    o_ref[...] = _dot_f32(c * jax.nn.sigmoid(c), w_ref[...]) + b_ref[...]


def _ada(cond, w_ada, b_ada):
    n = w_ada.shape[1]
    tn = 1024
    return pl.pallas_call(
        _ada_body,
        grid=(n // tn,),
        in_specs=[pl.BlockSpec((MOD_ROWS, D_MODEL), lambda j: (0, 0)),
                  pl.BlockSpec((D_MODEL, tn), lambda j: (0, j)),
                  pl.BlockSpec((1, tn), lambda j: (0, j))],
        out_specs=pl.BlockSpec((MOD_ROWS, tn), lambda j: (0, j)),
        out_shape=jax.ShapeDtypeStruct((MOD_ROWS, n), F32),
        compiler_params=_params(("arbitrary",), 32),
        name="ada",
    )(cond, w_ada, b_ada)


def _inproj_body(*refs, has_pos):
    if has_pos:
        x_ref, pos_ref, mod_ref, w_ref, rx_ref, gg_ref, hy_ref, ga_ref, gb_ref = refs
    else:
        x_ref, mod_ref, w_ref, rx_ref, gg_ref, hy_ref, ga_ref, gb_ref = refs
    x = x_ref[0]
    if has_pos:
        x = x + pos_ref[...]
    mod = mod_ref[0]
    sh1 = mod[:, 0:D_MODEL]
    sc1 = mod[:, D_MODEL:2 * D_MODEL]
    h = (x * (1.0 + sc1) + sh1).astype(BF16)
    o = 0
    rx_ref[...] = _dot(h, w_ref[:, o:o + D_RNN])
    o += D_RNN
    gg_ref[0] = _gelu(_dot(h, w_ref[:, o:o + D_RNN])).astype(BF16)
    o += D_RNN
    for j in range(3):
        hy_ref[0, :, j * D_HY:(j + 1) * D_HY] = _dot(h, w_ref[:, o:o + D_HY])
        o += D_HY
    ga_ref[0] = jax.nn.sigmoid(_dot(h, w_ref[:, o:o + D_MODEL])).astype(BF16)
    o += D_MODEL
    gb_ref[0] = jax.nn.sigmoid(_dot(h, w_ref[:, o:o + D_MODEL])).astype(BF16)


def _inproj(x, pos, mod3, w_in_bf, mod_row_of_batch, tl):
    b, l, _ = x.shape
    d_in = w_in_bf.shape[1]
    has_pos = pos is not None
    in_specs = [pl.BlockSpec((1, tl, D_MODEL), lambda bi, ti: (bi, ti, 0))]
    args = [x]
    if has_pos:
        in_specs.append(pl.BlockSpec((tl, D_MODEL), lambda bi, ti: (ti, 0)))
        args.append(pos)
    in_specs += [pl.BlockSpec((1, 1, 6 * D_MODEL), lambda bi, ti: (mod_row_of_batch(bi), 0, 0)),
                 pl.BlockSpec((D_MODEL, d_in), lambda bi, ti: (0, 0), pipeline_mode=pl.Buffered(1))]
    args += [mod3, w_in_bf]
    tok = lambda bi, ti: (bi, ti, 0)
    out_specs = [pl.BlockSpec((tl, D_RNN), lambda bi, ti: (ti, bi)),
                 pl.BlockSpec((1, tl, D_RNN), tok),
                 pl.BlockSpec((1, tl, 3 * D_HY), tok),
                 pl.BlockSpec((1, tl, D_MODEL), tok),
                 pl.BlockSpec((1, tl, D_MODEL), tok)]
    out_shape = [jax.ShapeDtypeStruct((l, b * D_RNN), F32),
                 jax.ShapeDtypeStruct((b, l, D_RNN), BF16),
                 jax.ShapeDtypeStruct((b, l, 3 * D_HY), F32),
                 jax.ShapeDtypeStruct((b, l, D_MODEL), BF16),
                 jax.ShapeDtypeStruct((b, l, D_MODEL), BF16)]
    return pl.pallas_call(
        functools.partial(_inproj_body, has_pos=has_pos),
        grid=(b, l // tl),
        in_specs=in_specs, out_specs=out_specs, out_shape=out_shape,
        compiler_params=_params(("parallel", "parallel"), 56),
        name="inproj",
    )(*args)


def _softplus(x):
    return jnp.maximum(x, 0.0) + jnp.log(1.0 + jnp.exp(-jnp.abs(x)))


def _rglru_body(rx_ref, h0_ref, wg_ref, bg_ref, lam_ref, cw_ref, cb_ref, y_ref, st_ref, a_s, u_s,
                *, seq, bb, tc):
    hd = RNN_HEAD_DIM
    nchunks = seq // tc
    cw = cw_ref[...]
    cb = cb_ref[...]
    for d in range(2):
        coef = -RGLRU_C * _softplus(-lam_ref[0, d:d + 1, :])
        wd = wg_ref[0, :, 2 * d * hd:2 * (d + 1) * hd]
        bd = bg_ref[0, :, 2 * d * hd:2 * (d + 1) * hd]

        def chunk(ci, h, d=d, coef=coef, wd=wd, bd=bd):
            c = ci if d == 0 else nchunks - 1 - ci
            t0 = pl.multiple_of(c * tc, tc)
            lo_ok = jnp.where(c > 0, 1.0, 0.0)
            hi_ok = jnp.where(c < nchunks - 1, 1.0, 0.0)
            lo = rx_ref[pl.ds(jnp.maximum(t0 - 2, 0), 2)] * lo_ok
            hi = rx_ref[pl.ds(jnp.minimum(t0 + tc, seq - 1), 1)] * hi_ok
            xe = jnp.concatenate([lo, rx_ref[pl.ds(t0, tc)], hi], axis=0)
            xc = cb[None] + sum(cw[k:k + 1][None] * xe[k:k + tc] for k in range(RNN_CONV_W))
            xc2 = xc.reshape(tc * bb, hd)
            g = _dot(xc2.astype(BF16), wd) + bd
            r = jax.nn.sigmoid(g[:, :hd])
            i = jax.nn.sigmoid(g[:, hd:])
            a = jnp.exp(coef * r)
            u = jnp.sqrt(1.0 - a * a) * (i * xc2)
            a_s[...] = a.reshape(tc, bb, hd)
            u_s[...] = u.reshape(tc, bb, hd)

            def step(j, h):
                tau = j if d == 0 else tc - 1 - j
                h = a_s[tau] * h + u_s[tau]
                if d == 0:
                    y_ref[t0 + tau] = h
                else:
                    y_ref[t0 + tau] = y_ref[t0 + tau] + h
                return h

            return lax.fori_loop(0, tc, step, h, unroll=8)

        h_fin = lax.fori_loop(0, nchunks, chunk, h0_ref[d])
        st_ref[d] = h_fin


def _rglru(rx3, h0, wg, bg, lam, cw, cb, bb):
    seq, b, _ = rx3.shape
    hd = RNN_HEAD_DIM
    tc = max(8, 512 // bb)
    blk = lambda bi, hi: (0, bi, hi)
    return pl.pallas_call(
        functools.partial(_rglru_body, seq=seq, bb=bb, tc=tc),
        grid=(b // bb, N_RNN_HEADS),
        in_specs=[pl.BlockSpec((seq, bb, hd), blk),
                  pl.BlockSpec((2, bb, hd), blk),
                  pl.BlockSpec((1, hd, 4 * hd), lambda bi, hi: (hi, 0, 0)),
                  pl.BlockSpec((1, 1, 4 * hd), lambda bi, hi: (hi, 0, 0)),
                  pl.BlockSpec((1, 2, hd), lambda bi, hi: (hi, 0, 0)),
                  pl.BlockSpec((RNN_CONV_W, hd), lambda bi, hi: (0, hi)),
                  pl.BlockSpec((1, hd), lambda bi, hi: (0, hi))],
        out_specs=[pl.BlockSpec((seq, bb, hd), blk),
                   pl.BlockSpec((2, bb, hd), blk)],
        out_shape=[jax.ShapeDtypeStruct((seq, b, D_RNN), F32),
                   jax.ShapeDtypeStruct((2, b, D_RNN), F32)],
        scratch_shapes=[pltpu.VMEM((tc, bb, hd), F32), pltpu.VMEM((tc, bb, hd), F32)],
        compiler_params=_params(("parallel", "parallel"), 56),
        name="rglru",
    )(rx3, h0, wg, bg, lam, cw, cb)


def _dft_matrices(seq):
    k = np.arange(seq, dtype=np.float64)[:, None]
    s = np.arange(seq, dtype=np.float64)[None, :]
    ang = np.pi * k * s / seq
    top = np.cos(ang)
    bot = -np.sin(ang)
    bot[0, :] = np.where(np.arange(seq) % 2 == 0, 1.0, -1.0)
    fwd = np.concatenate([top, bot], axis=0)
    return jnp.asarray(fwd, dtype=BF16), jnp.asarray(fwd.T, dtype=BF16)


def _filter_features(seq):
    t = np.arange(seq, dtype=np.float32)
    t_norm = t / np.float32(max(seq - 1, 1))
    w = (np.float32(2.0 * math.pi) * t / np.float32(seq)).astype(np.float32)
    bands = np.linspace(1e-4, HY_EMB_BANDS - 1, HY_EMB_BANDS, dtype=np.float32)
    fw = w[:, None] * bands[None, :]
    z = np.concatenate([t_norm[:, None], np.cos(fw), -np.sin(fw)], axis=-1).astype(np.float32)
    zp = np.zeros((seq, HY_EMB_PAD), np.float32)
    zp[:, :HY_EMB_DIM] = z
    deltas = np.abs(np.linspace(HY_MIN_DECAY, HY_MAX_DECAY, D_HY, dtype=np.float32))
    return jnp.asarray(zp), jnp.asarray(t_norm[:, None]), jnp.asarray(deltas[None, :])


def _dot_split(f_bf, x):
    hi = x.astype(BF16)
    lo = (x - hi.astype(F32)).astype(BF16)
    return _dot(f_bf, hi) + _dot(f_bf, lo)


def _hyfilt_body(z_ref, tn_ref, dl_ref, w1_ref, b1_ref, w2_ref, b2_ref, fr_ref, w3f_ref, b3f_ref,
                 w3b_ref, b3b_ref, f_ref, kr_ref, ki_ref, *, seq):
    freq = fr_ref[...]
    hid = jnp.sin(freq * (_dot_f32(z_ref[...], w1_ref[...]) + b1_ref[...]))
    hid = jnp.sin(freq * (_dot_f32(hid, w2_ref[...]) + b2_ref[...]))
    decay = jnp.exp(-tn_ref[...] * dl_ref[...])
    ff = (_dot_f32(hid, w3f_ref[...]) + b3f_ref[...]) * decay
    fb = (_dot_f32(hid, w3b_ref[...]) + b3b_ref[...]) * decay
    row = lax.broadcasted_iota(jnp.int32, ff.shape, 0)
    fb = jnp.where(row == 0, 0.0, fb)
    s = ff + fb
    dm = ff - fb
    sign = jnp.where((row & 1) == 0, 1.0, -1.0)
    nyq = jnp.sum(s * sign, axis=0, keepdims=True)
    kr = _dot_split(f_ref[0:seq, :], s)
    ki = _dot_split(f_ref[seq:2 * seq, :], dm)
    scale = jnp.where(row == 0, 0.5 / seq, 1.0 / seq)
    kr_ref[0] = kr * scale
    ki_ref[0] = jnp.where(row == 0, nyq, ki) * scale


def _hyfilt(seq, fwd, p):
    z, tn, dl = _filter_features(seq)
    dt = 512
    nd = D_HY // dt
    hidn = HY_FILTER_HIDDEN
    const = lambda o, j: (0, 0)
    w3 = p["hy_ffn_w3"]
    b3 = p["hy_ffn_b3"][None, :]
    in_specs = [pl.BlockSpec((seq, HY_EMB_PAD), const),
                pl.BlockSpec((seq, 1), const),
                pl.BlockSpec((1, dt), lambda o, j: (0, j)),
                pl.BlockSpec((HY_EMB_PAD, hidn), const),
                pl.BlockSpec((1, hidn), const),
                pl.BlockSpec((hidn, hidn), const),
                pl.BlockSpec((1, hidn), const),
                pl.BlockSpec((1, hidn), const),
                pl.BlockSpec((hidn, dt), lambda o, j: (0, o * nd + j)),
                pl.BlockSpec((1, dt), lambda o, j: (0, o * nd + j)),
                pl.BlockSpec((hidn, dt), lambda o, j: (0, (HY_ORDER + o) * nd + j)),
                pl.BlockSpec((1, dt), lambda o, j: (0, (HY_ORDER + o) * nd + j)),
                pl.BlockSpec((2 * seq, seq), const)]
    w1p = jnp.zeros((HY_EMB_PAD, hidn), F32).at[:HY_EMB_DIM].set(p["hy_ffn_w1"])
    out_spec = pl.BlockSpec((1, seq, dt), lambda o, j: (o, 0, j))
    return pl.pallas_call(
        functools.partial(_hyfilt_body, seq=seq),
        grid=(HY_ORDER, nd),
        in_specs=in_specs,
        out_specs=[out_spec, out_spec],
        out_shape=[jax.ShapeDtypeStruct((HY_ORDER, seq, D_HY), F32)] * 2,
        compiler_params=_params(("parallel", "parallel"), 48),
        name="hyfilt",
    )(z, tn, dl, w1p, p["hy_ffn_b1"][None, :], p["hy_ffn_w2"], p["hy_ffn_b2"][None, :],
      p["hy_sin_freq"][None, :], w3, b3, w3, b3, fwd)


def _hyena_body(x1_ref, x2_ref, v_ref, w1_ref, w2_ref, wv_ref, b1_ref, b2_ref, bv_ref,
                kr_ref, ki_ref, sk_ref, f_ref, ft_ref, o_ref, *, seq, lanes):
    dt = v_ref.shape[2]
    row = lax.broadcasted_iota(jnp.int32, (seq, lanes), 0)
    first = row == 0
    last = row == seq - 1

    def conv3(x_ref, w_ref, b_ref, cols):
        x = x_ref[0, :, cols]
        xm = jnp.where(first, 0.0, pltpu.roll(x, 1, 0))
        xp = jnp.where(last, 0.0, pltpu.roll(x, seq - 1, 0))
        return w_ref[0:1, cols] * xm + w_ref[1:2, cols] * x + w_ref[2:3, cols] * xp + b_ref[:, cols]

    def spectrum_product(uf, o, cols):
        kr = kr_ref[o, :, cols]
        ki = ki_ref[o, :, cols]
        kiz = jnp.where(first, 0.0, ki)
        krb = jnp.where(first, ki, kr)
        top = uf[:seq]
        bot = uf[seq:]
        return jnp.concatenate([top * kr - bot * kiz, top * kiz + bot * krb], axis=0).astype(BF16)

    chains = [slice(c * lanes, (c + 1) * lanes) for c in range(dt // lanes)]
    u = [conv3(v_ref, wv_ref, bv_ref, cols) for cols in chains]
    for o, gate_ref, gw_ref, gb_ref in ((0, x1_ref, w1_ref, b1_ref), (1, x2_ref, w2_ref, b2_ref)):
        uf = [_dot(f_ref[...], uc.astype(BF16)) for uc in u]
        y = []
        for cols, ufc in zip(chains, uf):
            y.append(_dot(ft_ref[...], spectrum_product(ufc, o, cols)))
        u = [conv3(gate_ref, gw_ref, gb_ref, cols) * (yc + uc * sk_ref[o:o + 1, cols])
             for cols, yc, uc in zip(chains, y, u)]
    for cols, z in zip(chains, u):
        o_ref[0, :, cols] = z.astype(BF16)


def _hyena(hy, cw, cb, kr, ki, skip, fwd, inv, dt, lanes):
    b, seq, _ = hy.shape
    nd = D_HY // dt
    const = lambda j, bi: (0, 0)
    part = lambda k: (lambda j, bi: (bi, 0, k * nd + j))
    wpart = lambda k: (lambda j, bi: (0, k * nd + j))
    once = pl.Buffered(1)
    in_specs = ([pl.BlockSpec((1, seq, dt), part(k)) for k in range(3)]
                + [pl.BlockSpec((3, dt), wpart(k)) for k in range(3)]
                + [pl.BlockSpec((1, dt), wpart(k)) for k in range(3)]
                + [pl.BlockSpec((HY_ORDER, seq, dt), lambda j, bi: (0, 0, j), pipeline_mode=once),
                   pl.BlockSpec((HY_ORDER, seq, dt), lambda j, bi: (0, 0, j), pipeline_mode=once),
                   pl.BlockSpec((HY_ORDER, dt), lambda j, bi: (0, j)),
                   pl.BlockSpec((2 * seq, seq), const, pipeline_mode=once),
                   pl.BlockSpec((seq, 2 * seq), const, pipeline_mode=once)])
    return pl.pallas_call(
        functools.partial(_hyena_body, seq=seq, lanes=min(dt, lanes)),
        grid=(nd, b),
        in_specs=in_specs,
        out_specs=pl.BlockSpec((1, seq, dt), lambda j, bi: (bi, 0, j)),
        out_shape=jax.ShapeDtypeStruct((b, seq, D_HY), BF16),
        compiler_params=_params(("parallel", "parallel"), 56),
        name="hyena",
    )(hy, hy, hy, cw, cw, cw, cb, cb, cb, kr, ki, skip, fwd, inv)


def _layer_norm(x, g, b):
    mu = jnp.mean(x, axis=-1, keepdims=True)
    xc = x - mu
    var = jnp.mean(xc * xc, axis=-1, keepdims=True)
    return xc * lax.rsqrt(var + LN_EPS) * g + b


def _merge_body(*refs, has_pos):
    if has_pos:
        (x_ref, pos_ref, mod_ref, yr_ref, gg_ref, z_ref, ga_ref, gb_ref, wa_ref, wb_ref, wo_ref,
         lg_ref, lb_ref, wq_ref, x1_ref, h2_ref, q_ref) = refs
    else:
        (x_ref, mod_ref, yr_ref, gg_ref, z_ref, ga_ref, gb_ref, wa_ref, wb_ref, wo_ref,
         lg_ref, lb_ref, wq_ref, x1_ref, h2_ref, q_ref) = refs
    d = D_MODEL
    mod = mod_ref[0]
    g1 = mod[:, 2 * d:3 * d]
    sh2 = mod[:, 3 * d:4 * d]
    sc2 = mod[:, 4 * d:5 * d]
    ya = _dot((yr_ref[...].astype(BF16) * gg_ref[0]), wa_ref[...])
    yb = _dot(z_ref[0], wb_ref[...])
    m = ga_ref[0].astype(F32) * ya + gb_ref[0].astype(F32) * yb
    y = _dot(m.astype(BF16), wo_ref[...])
    x = x_ref[0]
    if has_pos:
        x = x + pos_ref[...]
    x1 = _layer_norm(DN_ALPHA * x + g1 * y, lg_ref[...], lb_ref[...])
    x1_ref[0] = x1
    h2 = (x1 * (1.0 + sc2) + sh2).astype(BF16)
    h2_ref[0] = h2
    q_ref[0] = _dot(h2, wq_ref[...]).astype(BF16)


def _merge(x, pos, mod3, mod_row_of_batch, yr2, gg, z, ga, gb, wa, wb, wo, lg, lb, wq, tl):
    b, l, d = x.shape
    nq = wq.shape[1]
    has_pos = pos is not None
    tok = lambda bi, ti: (bi, ti, 0)
    const = lambda bi, ti: (0, 0)
    in_specs = [pl.BlockSpec((1, tl, d), tok)]
    args = [x]
    if has_pos:
        in_specs.append(pl.BlockSpec((tl, d), lambda bi, ti: (ti, 0)))
        args.append(pos)
    in_specs += [pl.BlockSpec((1, 1, 6 * d), lambda bi, ti: (mod_row_of_batch(bi), 0, 0)),
                 pl.BlockSpec((tl, D_RNN), lambda bi, ti: (ti, bi)),
                 pl.BlockSpec((1, tl, d), tok), pl.BlockSpec((1, tl, d), tok),
                 pl.BlockSpec((1, tl, d), tok), pl.BlockSpec((1, tl, d), tok),
                 pl.BlockSpec((d, d), const), pl.BlockSpec((d, d), const), pl.BlockSpec((d, d), const),
                 pl.BlockSpec((1, d), const), pl.BlockSpec((1, d), const),
                 pl.BlockSpec((d, nq), const)]
    args += [mod3, yr2, gg, z, ga, gb, wa, wb, wo, lg, lb, wq]
    return pl.pallas_call(
        functools.partial(_merge_body, has_pos=has_pos),
        grid=(b, l // tl),
        in_specs=in_specs,
        out_specs=[pl.BlockSpec((1, tl, d), tok), pl.BlockSpec((1, tl, d), tok),
                   pl.BlockSpec((1, tl, nq), tok)],
        out_shape=[jax.ShapeDtypeStruct((b, l, d), F32), jax.ShapeDtypeStruct((b, l, d), BF16),
                   jax.ShapeDtypeStruct((b, l, nq), BF16)],
        compiler_params=_params(("parallel", "parallel"), 56),
        name="merge",
    )(*args)


RANK_NONE = 1.0e9


def _topk_rank(problems, k, exact, want_rank):
    n_prob = len(problems)
    s = list(problems)
    ranks = [jnp.full(p.shape, RANK_NONE, F32) if w else None for p, w in zip(problems, want_rank)]
    vals = [[] for _ in range(n_prob)]
    for r in range(k):
        for i in range(n_prob):
            n = s[i].shape[0]
            m = jnp.max(s[i], axis=0, keepdims=True)
            hit = s[i] == m
            if exact:
                iota = lax.broadcasted_iota(jnp.int32, s[i].shape, 0).astype(F32)
                idx = jnp.min(jnp.where(hit, iota, float(n)), axis=0, keepdims=True)
                hit = iota == idx
            if want_rank[i]:
                ranks[i] = jnp.where(hit, float(r), ranks[i])
            s[i] = jnp.where(hit, NEG_INF, s[i])
            vals[i].append(m)
    taken = [jnp.where(a != b, 1.0, 0.0) for a, b in zip(s, problems)]
    if exact:
        ties = [jnp.zeros((1, p.shape[1]), F32) for p in problems]
    else:
        ties = [jnp.where(jnp.sum(t, axis=0, keepdims=True) > float(k), 1.0, 0.0) for t in taken]
    return vals, ranks, taken, ties


def _select_tiles(s1, s2, exact):
    k = PEER_TOPK
    nt = len(s1)
    vals, ranks, _, ties = _topk_rank(s1 + s2, k, exact, [exact] * nt + [True] * nt)
    v1, v2 = vals[:nt], vals[nt:]
    rank1, rank2 = ranks[:nt], ranks[nt:]
    cands = []
    for t in range(nt):
        v2_16 = jnp.concatenate(v2[t], axis=0)
        v2_8 = v2_16[0:8]
        v1_hi = jnp.concatenate(v1[t][8:16], axis=0)
        cands.append(jnp.concatenate([v1[t][0] + v2_16] + [v1[t][a] + v2_8 for a in range(1, 8)]
                                     + [v1_hi + v2[t][0]], axis=0))
    _, _, sels, tcs = _topk_rank(cands, k, exact, [False] * nt)
    outs = []
    for t in range(nt):
        sel, cand = sels[t], cands[t]
        z = jnp.sum(sel * jnp.exp(cand - cand[0:1]), axis=0, keepdims=True)
        counts = [jnp.sum(sel[0:16], axis=0, keepdims=True)]
        counts += [jnp.sum(sel[8 + 8 * a:16 + 8 * a], axis=0, keepdims=True) for a in range(1, 8)]
        counts += [sel[72 + a:73 + a] for a in range(8)]
        ni = jnp.zeros(s1[t].shape, F32)
        for a in range(k):
            is_a = (rank1[t] == float(a)) if exact else (s1[t] == v1[t][a])
            ni = jnp.where(is_a, counts[a], ni)
        e1n = jnp.exp(s1[t] - v1[t][0]) / z
        e2 = jnp.exp(s2[t] - v2[t][0])
        outs.append((e1n, ni, e2.astype(BF16), rank2[t].astype(BF16), ties[t] + ties[nt + t] + tcs[t]))
    return outs


def _peersel_body(q_ref, k_ref, e1n_ref, ni_ref, e2_ref, r2_ref, s1_ref, s2_ref):
    nt = (((1,), (1,)), ((), ()))
    s1_ref[...] = lax.dot_general(k_ref[0], q_ref[:, 0:PEER_HALF], nt, preferred_element_type=F32)
    s2_ref[...] = lax.dot_general(k_ref[1], q_ref[:, PEER_HALF:2 * PEER_HALF], nt, preferred_element_type=F32)
    lane = 128
    n_tiles = s1_ref.shape[1] // lane

    def run(tiles, exact):
        outs = _select_tiles([s1_ref[:, c] for c in tiles], [s2_ref[:, c] for c in tiles], exact)
        for cols, (e1n, ni, e2, r2, _) in zip(tiles, outs):
            e1n_ref[0, :, cols] = e1n
            ni_ref[0, :, cols] = ni
            e2_ref[0, :, cols] = e2
            r2_ref[0, :, cols] = r2
        return [o[4] for o in outs]

    tiles = [slice(lt * lane, (lt + 1) * lane) for lt in range(n_tiles)]
    ties = run(tiles, exact=False)
    for cols, t in zip(tiles, ties):
        @pl.when(jnp.max(t) > 0.0)
        def _(cols=cols):
            run([cols], exact=True)


def _peersel(q2, keys_bf, tt):
    t = q2.shape[0]
    out_spec = pl.BlockSpec((1, N_KEYS, tt), lambda ti, h: (h, 0, ti))
    return pl.pallas_call(
        _peersel_body,
        grid=(t // tt, PEER_HEADS),
        in_specs=[pl.BlockSpec((tt, 2 * PEER_HALF), lambda ti, h: (ti, h)),
                  pl.BlockSpec((2, N_KEYS, PEER_HALF), lambda ti, h: (0, 0, 0))],
        out_specs=[out_spec] * 4,
        out_shape=[jax.ShapeDtypeStruct((PEER_HEADS, N_KEYS, t), F32)] * 2
        + [jax.ShapeDtypeStruct((PEER_HEADS, N_KEYS, t), BF16)] * 2,
        scratch_shapes=[pltpu.VMEM((N_KEYS, tt), F32), pltpu.VMEM((N_KEYS, tt), F32)],
        compiler_params=_params(("parallel", "parallel"), 48),
        name="peersel",
    )(q2, keys_bf)


PEER_EB = 2048
PEER_EG = 512


def _gelu_sigmoid(x):
    k1 = 2.0 * math.sqrt(2.0 / math.pi) * math.log2(math.e)
    k2 = k1 * 0.044715
    z = x * (-k1 - k2 * (x * x))
    return x * (1.0 / (1.0 + jnp.exp2(z)))


def _peermix_body(h2_ref, u_ref, vt_ref, e1n_ref, ni_ref, e2_ref, r2_ref, x1_ref, mod_ref, lg_ref, lb_ref,
                  o_ref, acc_ref, w_ref, act_ref, g_ref, h2t_ref):
    e = pl.program_id(1)
    tt = acc_ref.shape[1]
    n_groups = PEER_EB // PEER_EG
    slabs_per_group = PEER_EG // N_KEYS
    pack = 16
    n_chunks = N_KEYS // pack
    lanes = 256
    zero = jnp.zeros((), BF16)

    @pl.when(e == 0)
    def _():
        acc_ref[...] = jnp.zeros(acc_ref.shape, F32)
        h2t_ref[...] = h2_ref[...].T

    def selection_weights(s):
        for lh in range(tt // lanes):
            cols = slice(lh * lanes, (lh + 1) * lanes)
            g = [None] * n_chunks
            for h in range(PEER_HEADS):
                nrow = jnp.broadcast_to(ni_ref[h, s:s + 1, cols], (pack, lanes)).astype(BF16)
                erow = jnp.broadcast_to(e1n_ref[h, s:s + 1, cols], (pack, lanes)).astype(BF16)
                for c in range(n_chunks):
                    rows = slice(c * pack, (c + 1) * pack)
                    term = jnp.where(r2_ref[h, rows, cols] < nrow, e2_ref[h, rows, cols], zero) * erow
                    g[c] = term if g[c] is None else g[c] + term
            for c in range(n_chunks):
                g_ref[s * N_KEYS + c * pack:s * N_KEYS + (c + 1) * pack, cols] = g[c]

    for p in range(n_groups):
        grp = slice(p * PEER_EG, (p + 1) * PEER_EG)
        act_ref[grp, :] = _dot(u_ref[grp, :], h2t_ref[...])
        for s in range(p * slabs_per_group, (p + 1) * slabs_per_group):
            selection_weights(s)
    for p in range(n_groups):
        grp = slice(p * PEER_EG, (p + 1) * PEER_EG)
        for c in range(PEER_EG // pack):
            rows = slice(p * PEER_EG + c * pack, p * PEER_EG + (c + 1) * pack)
            w_ref[rows, :] = _gelu_sigmoid(act_ref[rows, :]).astype(BF16) * g_ref[rows, :]
        acc_ref[...] += _dot(vt_ref[:, grp], w_ref[grp, :])

    @pl.when(e == pl.num_programs(1) - 1)
    def _():
        d = D_MODEL
        g2 = mod_ref[0][:, 5 * d:6 * d]
        y = acc_ref[...].T
        o_ref[...] = _layer_norm(DN_ALPHA * x1_ref[...] + g2 * y, lg_ref[...], lb_ref[...])


def _peermix(h2, u_bf, vt_bf, e1n, ni, e2, r2, x1, mod3, mod_row_of_tile, lg, lb, tt):
    t, d = h2.shape
    nb = u_bf.shape[0] // PEER_EB
    ns = PEER_EB // N_KEYS
    tile = lambda ti, e: (ti, 0)
    const = lambda ti, e: (0, 0)
    return pl.pallas_call(
        _peermix_body,
        grid=(t // tt, nb),
        in_specs=[pl.BlockSpec((tt, d), tile),
                  pl.BlockSpec((PEER_EB, d), lambda ti, e: (e, 0)),
                  pl.BlockSpec((d, PEER_EB), lambda ti, e: (0, e)),
                  pl.BlockSpec((PEER_HEADS, ns, tt), lambda ti, e: (0, e, ti)),
                  pl.BlockSpec((PEER_HEADS, ns, tt), lambda ti, e: (0, e, ti)),
                  pl.BlockSpec((PEER_HEADS, N_KEYS, tt), lambda ti, e: (0, 0, ti)),
                  pl.BlockSpec((PEER_HEADS, N_KEYS, tt), lambda ti, e: (0, 0, ti)),
                  pl.BlockSpec((tt, d), tile),
                  pl.BlockSpec((1, 1, 6 * d), lambda ti, e: (mod_row_of_tile(ti), 0, 0)),
                  pl.BlockSpec((1, d), const), pl.BlockSpec((1, d), const)],
        out_specs=pl.BlockSpec((tt, d), tile),
        out_shape=jax.ShapeDtypeStruct((t, d), F32),
        scratch_shapes=[pltpu.VMEM((d, tt), F32), pltpu.VMEM((PEER_EB, tt), BF16),
                        pltpu.VMEM((PEER_EB, tt), F32), pltpu.VMEM((PEER_EB, tt), BF16),
                        pltpu.VMEM((d, tt), BF16)],
        compiler_params=_params(("parallel", "arbitrary"), 56),
        name="peermix",
    )(h2, u_bf, vt_bf, e1n, ni, e2, r2, x1, mod3, lg, lb)


def _prep_params(p):
    hd = RNN_HEAD_DIM
    q = dict(p)
    q["w_in_bf"] = p["w_in"].astype(BF16)
    q["wg"] = jnp.transpose(p["rnn_gate_w"], (2, 3, 0, 1, 4)).reshape(N_RNN_HEADS, hd, 4 * hd).astype(BF16)
    q["bg"] = jnp.transpose(p["rnn_gate_b"].reshape(2, 2, N_RNN_HEADS, hd), (2, 0, 1, 3)).reshape(
        N_RNN_HEADS, 1, 4 * hd)
    q["lam"] = jnp.transpose(p["rnn_lambda"].reshape(2, N_RNN_HEADS, hd), (1, 0, 2))
    q["rnn_cb"] = p["rnn_conv_b"][None, :]
    q["hy_cb"] = p["hy_conv_b"][None, :]
    q["wa"] = p["w_branch_a"].astype(BF16)
    q["wb"] = p["w_branch_b"].astype(BF16)
    q["wo"] = p["w_out"].astype(BF16)
    q["wq"] = p["peer_w_query"].astype(BF16)
    q["ln1_g2"] = p["ln1_g"][None, :]
    q["ln1_b2"] = p["ln1_b"][None, :]
    q["ln2_g2"] = p["ln2_g"][None, :]
    q["ln2_b2"] = p["ln2_b"][None, :]
    return q


def _mixer_group(x, pos, mod3, mod_row_of_batch, h0, q, tl, bb, dt, hy_lanes):
    b, l, _ = x.shape
    fwd, inv = _dft_matrices(l)
    kr, ki = _hyfilt(l, fwd, q)
    rx2, gg, hy, ga, gb = _inproj(x, pos, mod3, q["w_in_bf"], mod_row_of_batch, tl)
    y3, st = _rglru(rx2.reshape(l, b, D_RNN), h0, q["wg"], q["bg"], q["lam"], q["rnn_conv_w"],
                    q["rnn_cb"], bb)
    z = _hyena(hy, q["hy_conv_w"], q["hy_cb"], kr, ki, q["hy_skip"], fwd, inv, dt, hy_lanes)
    x1, h2, qq = _merge(x, pos, mod3, mod_row_of_batch, y3.reshape(l, b * D_RNN), gg, z, ga, gb,
                        q["wa"], q["wb"], q["wo"], q["ln1_g2"], q["ln1_b2"], q["wq"], tl)
    return x1, h2, qq, st


def _group_transposed(v):
    return v.astype(BF16).T


def _peer_group(x1, h2, qq, mod3, mod_row_of_tile, q, tt_sel, tt_mix):
    b, l, d = x1.shape
    t = b * l
    e1n, ni, e2, r2 = _peersel(qq.reshape(t, qq.shape[-1]), q["keys_bf"], tt_sel)
    out = _peermix(h2.reshape(t, d), q["u_bf"], q["vt_bf"], e1n, ni, e2, r2, x1.reshape(t, d), mod3,
                   mod_row_of_tile, q["ln2_g2"], q["ln2_b2"], tt_mix)
    return out.reshape(b, l, d)


def _grid_pos_embed(n_tokens):
    rows = n_tokens // GRID_W
    t = np.arange(rows * GRID_W)
    r = (t // GRID_W).astype(np.float32)
    col = (t % GRID_W).astype(np.float32)
    quarter = D_MODEL // 4
    omega = (1.0 / (10000.0 ** (np.arange(quarter, dtype=np.float32) / np.float32(quarter)))).astype(np.float32)
    er = r[:, None] * omega[None, :]
    ec = col[:, None] * omega[None, :]
    return jnp.asarray(np.concatenate([np.sin(er), np.cos(er), np.sin(ec), np.cos(ec)], axis=-1), dtype=F32)


PARAM_NAMES = ("w_ada", "b_ada", "w_in", "rnn_conv_w", "rnn_conv_b", "rnn_gate_w", "rnn_gate_b", "rnn_lambda",
               "hy_conv_w", "hy_conv_b", "hy_ffn_w1", "hy_ffn_b1", "hy_ffn_w2", "hy_ffn_b2", "hy_ffn_w3",
               "hy_ffn_b3", "hy_sin_freq", "hy_skip", "w_branch_a", "w_branch_b", "w_out", "ln1_g", "ln1_b",
               "ln2_g", "ln2_b", "peer_w_query", "peer_sub_keys", "peer_u", "peer_v")


def kernel(x_prompt, x_sample, state_rglru, c, c_ctx, w_ada, b_ada, w_in, rnn_conv_w, rnn_conv_b, rnn_gate_w,
           rnn_gate_b, rnn_lambda, hy_conv_w, hy_conv_b, hy_ffn_w1, hy_ffn_b1, hy_ffn_w2, hy_ffn_b2, hy_ffn_w3,
           hy_ffn_b3, hy_sin_freq, hy_skip, w_branch_a, w_branch_b, w_out, ln1_g, ln1_b, ln2_g, ln2_b,
           peer_w_query, peer_sub_keys, peer_u, peer_v):
    stacked = dict(zip(PARAM_NAMES, (w_ada, b_ada, w_in, rnn_conv_w, rnn_conv_b, rnn_gate_w, rnn_gate_b,
                                     rnn_lambda, hy_conv_w, hy_conv_b, hy_ffn_w1, hy_ffn_b1, hy_ffn_w2,
                                     hy_ffn_b2, hy_ffn_w3, hy_ffn_b3, hy_sin_freq, hy_skip, w_branch_a,
                                     w_branch_b, w_out, ln1_g, ln1_b, ln2_g, ln2_b, peer_w_query,
                                     peer_sub_keys, peer_u, peer_v)))
    depth = w_ada.shape[0]
    bp, lp, d = x_prompt.shape
    bs, ls, _ = x_sample.shape
    assert bs + 1 <= MOD_ROWS
    cond = jnp.zeros((MOD_ROWS, d), F32).at[0].set(c_ctx).at[1:1 + bs].set(c)
    pos = _grid_pos_embed(ls)
    tt_mix = 512
    tt_sel = 512
    sample_tiles_per_batch = ls // tt_mix

    xp, xs = x_prompt, x_sample
    ctx_states = []
    for layer in range(depth):
        q = _prep_params({name: w[layer] for name, w in stacked.items()})
        q["keys_bf"] = q["peer_sub_keys"].astype(BF16)
        q["u_bf"] = q["peer_u"].astype(BF16)
        q["vt_bf"] = _group_transposed(q["peer_v"])
        mod3 = _ada(cond, q["w_ada"], q["b_ada"][None, :]).reshape(MOD_ROWS, 1, 6 * d)

        x1, h2, qq, st = _mixer_group(xp, None, mod3, lambda bi: 0, jnp.zeros((2, bp, D_RNN), F32), q,
                                      tl=lp, bb=32, dt=1024, hy_lanes=512)
        xp = _peer_group(x1, h2, qq, mod3, lambda ti: 0, q, tt_sel, tt_mix)
        ctx_states.append(jnp.transpose(st, (1, 0, 2)))

        pos_l = pos if layer == 0 else None
        h0 = jnp.transpose(state_rglru[:, layer], (1, 0, 2))
        x1, h2, qq, _ = _mixer_group(xs, pos_l, mod3, lambda bi: bi + 1, h0, q, tl=512, bb=8, dt=512, hy_lanes=256)
        xs = _peer_group(x1, h2, qq, mod3, lambda ti: 1 + ti // sample_tiles_per_batch, q, tt_sel, tt_mix)

    new_state = jnp.stack(ctx_states, axis=1).astype(x_prompt.dtype)
    return (xp, xs, new_state)
```

```python
import functools
import math

import numpy as np
import jax
import jax.numpy as jnp
from jax import lax
from jax.experimental import pallas as pl
from jax.experimental.pallas import tpu as pltpu

F32 = jnp.float32
BF16 = jnp.bfloat16

D_MODEL = 1024
D_RNN = 1024
N_RNN_HEADS = 4
RNN_HEAD_DIM = D_RNN // N_RNN_HEADS
RNN_CONV_W = 4
RGLRU_C = 8.0
D_HY = 1024
HY_ORDER = 2
HY_EMB_BANDS = 16
HY_EMB_DIM = 1 + 2 * HY_EMB_BANDS
HY_EMB_PAD = 64
HY_FILTER_HIDDEN = 64
HY_DECAY_TARGET = 1e-2
HY_MIN_DECAY = math.log(HY_DECAY_TARGET) / 1.5
HY_MAX_DECAY = math.log(HY_DECAY_TARGET) / 0.3
GRID_W = 64
N_KEYS = 128
N_EXPERTS = N_KEYS * N_KEYS
PEER_HEADS = 8
PEER_HALF = 128
PEER_TOPK = 16
DEPTH = 1
DN_ALPHA = (2.0 * DEPTH) ** 0.25
LN_EPS = 1e-5

MOD_ROWS = 16
MIB = 1024 * 1024
NEG_INF = float("-inf")

V7X_VMEM_MIB = 64
VMEM_RESERVE_MIB = 8
VMEM_LIMIT_MIB = V7X_VMEM_MIB - VMEM_RESERVE_MIB
V7X_LANES = 128
V7X_MXU_DIM = 256


def _params(semantics, vmem_mib=VMEM_LIMIT_MIB):
    return pltpu.CompilerParams(dimension_semantics=semantics, vmem_limit_bytes=vmem_mib * MIB)


def _gelu(x):
    return jax.nn.gelu(x, approximate=True)


def _dot(a, b):
    return jnp.dot(a, b, preferred_element_type=F32)


def _dot_f32(a, b):
    return jnp.dot(a, b, preferred_element_type=F32, precision=lax.Precision.HIGHEST)


def _ada_body(c_ref, w_ref, b_ref, o_ref):
    c = c_ref[...]
    o_ref[...] = _dot_f32(c * jax.nn.sigmoid(c), w_ref[...]) + b_ref[...]


def _ada(cond, w_ada, b_ada):
    n = w_ada.shape[1]
    tn = 1024
    return pl.pallas_call(
        _ada_body,
        grid=(n // tn,),
        in_specs=[pl.BlockSpec((MOD_ROWS, D_MODEL), lambda j: (0, 0)),
                  pl.BlockSpec((D_MODEL, tn), lambda j: (0, j)),
                  pl.BlockSpec((1, tn), lambda j: (0, j))],
        out_specs=pl.BlockSpec((MOD_ROWS, tn), lambda j: (0, j)),
        out_shape=jax.ShapeDtypeStruct((MOD_ROWS, n), F32),
        compiler_params=_params(("arbitrary",), V7X_VMEM_MIB // 2),
        name="ada",
    )(cond, w_ada, b_ada)


def _inproj_body(*refs, has_pos):
    if has_pos:
        x_ref, pos_ref, mod_ref, w_ref, rx_ref, gg_ref, hy_ref, ga_ref, gb_ref = refs
    else:
        x_ref, mod_ref, w_ref, rx_ref, gg_ref, hy_ref, ga_ref, gb_ref = refs
    x = x_ref[0]
    if has_pos:
        x = x + pos_ref[...]
    mod = mod_ref[0]
    sh1 = mod[:, 0:D_MODEL]
    sc1 = mod[:, D_MODEL:2 * D_MODEL]
    h = (x * (1.0 + sc1) + sh1).astype(BF16)
    o = 0
    rx_ref[...] = _dot(h, w_ref[:, o:o + D_RNN])
    o += D_RNN
    gg_ref[0] = _gelu(_dot(h, w_ref[:, o:o + D_RNN])).astype(BF16)
    o += D_RNN
    for j in range(3):
        hy_ref[0, :, j * D_HY:(j + 1) * D_HY] = _dot(h, w_ref[:, o:o + D_HY])
        o += D_HY
    ga_ref[0] = jax.nn.sigmoid(_dot(h, w_ref[:, o:o + D_MODEL])).astype(BF16)
    o += D_MODEL
    gb_ref[0] = jax.nn.sigmoid(_dot(h, w_ref[:, o:o + D_MODEL])).astype(BF16)


def _inproj(x, pos, mod3, w_in_bf, mod_row_of_batch, tl):
    b, l, _ = x.shape
    d_in = w_in_bf.shape[1]
    has_pos = pos is not None
    in_specs = [pl.BlockSpec((1, tl, D_MODEL), lambda bi, ti: (bi, ti, 0))]
    args = [x]
    if has_pos:
        in_specs.append(pl.BlockSpec((tl, D_MODEL), lambda bi, ti: (ti, 0)))
        args.append(pos)
    in_specs += [pl.BlockSpec((1, 1, 6 * D_MODEL), lambda bi, ti: (mod_row_of_batch(bi), 0, 0)),
                 pl.BlockSpec((D_MODEL, d_in), lambda bi, ti: (0, 0), pipeline_mode=pl.Buffered(1))]
    args += [mod3, w_in_bf]
    tok = lambda bi, ti: (bi, ti, 0)
    out_specs = [pl.BlockSpec((tl, D_RNN), lambda bi, ti: (ti, bi)),
                 pl.BlockSpec((1, tl, D_RNN), tok),
                 pl.BlockSpec((1, tl, 3 * D_HY), tok),
                 pl.BlockSpec((1, tl, D_MODEL), tok),
                 pl.BlockSpec((1, tl, D_MODEL), tok)]
    out_shape = [jax.ShapeDtypeStruct((l, b * D_RNN), F32),
                 jax.ShapeDtypeStruct((b, l, D_RNN), BF16),
                 jax.ShapeDtypeStruct((b, l, 3 * D_HY), F32),
                 jax.ShapeDtypeStruct((b, l, D_MODEL), BF16),
                 jax.ShapeDtypeStruct((b, l, D_MODEL), BF16)]
    return pl.pallas_call(
        functools.partial(_inproj_body, has_pos=has_pos),
        grid=(b, l // tl),
        in_specs=in_specs, out_specs=out_specs, out_shape=out_shape,
        compiler_params=_params(("parallel", "parallel")),
        name="inproj",
    )(*args)


def _softplus(x):
    return jnp.maximum(x, 0.0) + jnp.log(1.0 + jnp.exp(-jnp.abs(x)))


def _rglru_body(rx_ref, h0_ref, wg_ref, bg_ref, lam_ref, cw_ref, cb_ref, y_ref, st_ref, a_s, u_s,
                *, seq, bb, tc):
    hd = RNN_HEAD_DIM
    nchunks = seq // tc
    cw = cw_ref[...]
    cb = cb_ref[...]
    for d in range(2):
        coef = -RGLRU_C * _softplus(-lam_ref[0, d:d + 1, :])
        wd = wg_ref[0, :, 2 * d * hd:2 * (d + 1) * hd]
        bd = bg_ref[0, :, 2 * d * hd:2 * (d + 1) * hd]

        def chunk(ci, h, d=d, coef=coef, wd=wd, bd=bd):
            c = ci if d == 0 else nchunks - 1 - ci
            t0 = pl.multiple_of(c * tc, tc)
            lo_ok = jnp.where(c > 0, 1.0, 0.0)
            hi_ok = jnp.where(c < nchunks - 1, 1.0, 0.0)
            lo = rx_ref[pl.ds(jnp.maximum(t0 - 2, 0), 2)] * lo_ok
            hi = rx_ref[pl.ds(jnp.minimum(t0 + tc, seq - 1), 1)] * hi_ok
            xe = jnp.concatenate([lo, rx_ref[pl.ds(t0, tc)], hi], axis=0)
            xc = cb[None] + sum(cw[k:k + 1][None] * xe[k:k + tc] for k in range(RNN_CONV_W))
            xc2 = xc.reshape(tc * bb, hd)
            g = _dot(xc2.astype(BF16), wd) + bd
            r = jax.nn.sigmoid(g[:, :hd])
            i = jax.nn.sigmoid(g[:, hd:])
            a = jnp.exp(coef * r)
            u = jnp.sqrt(1.0 - a * a) * (i * xc2)
            a_s[...] = a.reshape(tc, bb, hd)
            u_s[...] = u.reshape(tc, bb, hd)

            def step(j, h):
                tau = j if d == 0 else tc - 1 - j
                h = a_s[tau] * h + u_s[tau]
                if d == 0:
                    y_ref[t0 + tau] = h
                else:
                    y_ref[t0 + tau] = y_ref[t0 + tau] + h
                return h

            return lax.fori_loop(0, tc, step, h, unroll=8)

        h_fin = lax.fori_loop(0, nchunks, chunk, h0_ref[d])
        st_ref[d] = h_fin


def _rglru(rx3, h0, wg, bg, lam, cw, cb, bb):
    seq, b, _ = rx3.shape
    hd = RNN_HEAD_DIM
    tc = max(8, 512 // bb)
    blk = lambda bi, hi: (0, bi, hi)
    return pl.pallas_call(
        functools.partial(_rglru_body, seq=seq, bb=bb, tc=tc),
        grid=(b // bb, N_RNN_HEADS),
        in_specs=[pl.BlockSpec((seq, bb, hd), blk),
                  pl.BlockSpec((2, bb, hd), blk),
                  pl.BlockSpec((1, hd, 4 * hd), lambda bi, hi: (hi, 0, 0)),
                  pl.BlockSpec((1, 1, 4 * hd), lambda bi, hi: (hi, 0, 0)),
                  pl.BlockSpec((1, 2, hd), lambda bi, hi: (hi, 0, 0)),
                  pl.BlockSpec((RNN_CONV_W, hd), lambda bi, hi: (0, hi)),
                  pl.BlockSpec((1, hd), lambda bi, hi: (0, hi))],
        out_specs=[pl.BlockSpec((seq, bb, hd), blk),
                   pl.BlockSpec((2, bb, hd), blk)],
        out_shape=[jax.ShapeDtypeStruct((seq, b, D_RNN), F32),
                   jax.ShapeDtypeStruct((2, b, D_RNN), F32)],
        scratch_shapes=[pltpu.VMEM((tc, bb, hd), F32), pltpu.VMEM((tc, bb, hd), F32)],
        compiler_params=_params(("parallel", "parallel")),
        name="rglru",
    )(rx3, h0, wg, bg, lam, cw, cb)


def _dft_matrices(seq):
    k = np.arange(seq, dtype=np.float64)[:, None]
    s = np.arange(seq, dtype=np.float64)[None, :]
    ang = np.pi * k * s / seq
    top = np.cos(ang)
    bot = -np.sin(ang)
    bot[0, :] = np.where(np.arange(seq) % 2 == 0, 1.0, -1.0)
    fwd = np.concatenate([top, bot], axis=0)
    return jnp.asarray(fwd, dtype=BF16), jnp.asarray(fwd.T, dtype=BF16)


def _filter_features(seq):
    t = np.arange(seq, dtype=np.float32)
    t_norm = t / np.float32(max(seq - 1, 1))
    w = (np.float32(2.0 * math.pi) * t / np.float32(seq)).astype(np.float32)
    bands = np.linspace(1e-4, HY_EMB_BANDS - 1, HY_EMB_BANDS, dtype=np.float32)
    fw = w[:, None] * bands[None, :]
    z = np.concatenate([t_norm[:, None], np.cos(fw), -np.sin(fw)], axis=-1).astype(np.float32)
    zp = np.zeros((seq, HY_EMB_PAD), np.float32)
    zp[:, :HY_EMB_DIM] = z
    deltas = np.abs(np.linspace(HY_MIN_DECAY, HY_MAX_DECAY, D_HY, dtype=np.float32))
    return jnp.asarray(zp), jnp.asarray(t_norm[:, None]), jnp.asarray(deltas[None, :])


def _dot_split(f_bf, x):
    hi = x.astype(BF16)
    lo = (x - hi.astype(F32)).astype(BF16)
    return _dot(f_bf, hi) + _dot(f_bf, lo)


def _hyfilt_body(z_ref, tn_ref, dl_ref, w1_ref, b1_ref, w2_ref, b2_ref, fr_ref, w3f_ref, b3f_ref,
                 w3b_ref, b3b_ref, f_ref, kr_ref, ki_ref, *, seq):
    freq = fr_ref[...]
    hid = jnp.sin(freq * (_dot_f32(z_ref[...], w1_ref[...]) + b1_ref[...]))
    hid = jnp.sin(freq * (_dot_f32(hid, w2_ref[...]) + b2_ref[...]))
    decay = jnp.exp(-tn_ref[...] * dl_ref[...])
    ff = (_dot_f32(hid, w3f_ref[...]) + b3f_ref[...]) * decay
    fb = (_dot_f32(hid, w3b_ref[...]) + b3b_ref[...]) * decay
    row = lax.broadcasted_iota(jnp.int32, ff.shape, 0)
    fb = jnp.where(row == 0, 0.0, fb)
    s = ff + fb
    dm = ff - fb
    sign = jnp.where((row & 1) == 0, 1.0, -1.0)
    nyq = jnp.sum(s * sign, axis=0, keepdims=True)
    kr = _dot_split(f_ref[0:seq, :], s)
    ki = _dot_split(f_ref[seq:2 * seq, :], dm)
    scale = jnp.where(row == 0, 0.5 / seq, 1.0 / seq)
    kr_ref[0] = kr * scale
    ki_ref[0] = jnp.where(row == 0, nyq, ki) * scale


def _hyfilt(seq, fwd, p):
    z, tn, dl = _filter_features(seq)
    dt = 512
    nd = D_HY // dt
    hidn = HY_FILTER_HIDDEN
    const = lambda o, j: (0, 0)
    w3 = p["hy_ffn_w3"]
    b3 = p["hy_ffn_b3"][None, :]
    in_specs = [pl.BlockSpec((seq, HY_EMB_PAD), const),
                pl.BlockSpec((seq, 1), const),
                pl.BlockSpec((1, dt), lambda o, j: (0, j)),
                pl.BlockSpec((HY_EMB_PAD, hidn), const),
                pl.BlockSpec((1, hidn), const),
                pl.BlockSpec((hidn, hidn), const),
                pl.BlockSpec((1, hidn), const),
                pl.BlockSpec((1, hidn), const),
                pl.BlockSpec((hidn, dt), lambda o, j: (0, o * nd + j)),
                pl.BlockSpec((1, dt), lambda o, j: (0, o * nd + j)),
                pl.BlockSpec((hidn, dt), lambda o, j: (0, (HY_ORDER + o) * nd + j)),
                pl.BlockSpec((1, dt), lambda o, j: (0, (HY_ORDER + o) * nd + j)),
                pl.BlockSpec((2 * seq, seq), const)]
    w1p = jnp.zeros((HY_EMB_PAD, hidn), F32).at[:HY_EMB_DIM].set(p["hy_ffn_w1"])
    out_spec = pl.BlockSpec((1, seq, dt), lambda o, j: (o, 0, j))
    return pl.pallas_call(
        functools.partial(_hyfilt_body, seq=seq),
        grid=(HY_ORDER, nd),
        in_specs=in_specs,
        out_specs=[out_spec, out_spec],
        out_shape=[jax.ShapeDtypeStruct((HY_ORDER, seq, D_HY), F32)] * 2,
        compiler_params=_params(("parallel", "parallel"), VMEM_LIMIT_MIB - VMEM_RESERVE_MIB),
        name="hyfilt",
    )(z, tn, dl, w1p, p["hy_ffn_b1"][None, :], p["hy_ffn_w2"], p["hy_ffn_b2"][None, :],
      p["hy_sin_freq"][None, :], w3, b3, w3, b3, fwd)


def _hyena_body(x1_ref, x2_ref, v_ref, w1_ref, w2_ref, wv_ref, b1_ref, b2_ref, bv_ref,
                kr_ref, ki_ref, sk_ref, f_ref, ft_ref, o_ref, *, seq, lanes):
    dt = v_ref.shape[2]
    row = lax.broadcasted_iota(jnp.int32, (seq, lanes), 0)
    first = row == 0
    last = row == seq - 1

    def conv3(x_ref, w_ref, b_ref, cols):
        x = x_ref[0, :, cols]
        xm = jnp.where(first, 0.0, pltpu.roll(x, 1, 0))
        xp = jnp.where(last, 0.0, pltpu.roll(x, seq - 1, 0))
        return w_ref[0:1, cols] * xm + w_ref[1:2, cols] * x + w_ref[2:3, cols] * xp + b_ref[:, cols]

    def spectrum_product(uf, o, cols):
        kr = kr_ref[o, :, cols]
        ki = ki_ref[o, :, cols]
        kiz = jnp.where(first, 0.0, ki)
        krb = jnp.where(first, ki, kr)
        top = uf[:seq]
        bot = uf[seq:]
        return jnp.concatenate([top * kr - bot * kiz, top * kiz + bot * krb], axis=0).astype(BF16)

    chains = [slice(c * lanes, (c + 1) * lanes) for c in range(dt // lanes)]
    u = [conv3(v_ref, wv_ref, bv_ref, cols) for cols in chains]
    for o, gate_ref, gw_ref, gb_ref in ((0, x1_ref, w1_ref, b1_ref), (1, x2_ref, w2_ref, b2_ref)):
        uf = [_dot(f_ref[...], uc.astype(BF16)) for uc in u]
        y = []
        for cols, ufc in zip(chains, uf):
            y.append(_dot(ft_ref[...], spectrum_product(ufc, o, cols)))
        u = [conv3(gate_ref, gw_ref, gb_ref, cols) * (yc + uc * sk_ref[o:o + 1, cols])
             for cols, yc, uc in zip(chains, y, u)]
    for cols, z in zip(chains, u):
        o_ref[0, :, cols] = z.astype(BF16)


def _hyena(hy, cw, cb, kr, ki, skip, fwd, inv, dt, lanes):
    b, seq, _ = hy.shape
    nd = D_HY // dt
    const = lambda j, bi: (0, 0)
    part = lambda k: (lambda j, bi: (bi, 0, k * nd + j))
    wpart = lambda k: (lambda j, bi: (0, k * nd + j))
    once = pl.Buffered(1)
    in_specs = ([pl.BlockSpec((1, seq, dt), part(k)) for k in range(3)]
                + [pl.BlockSpec((3, dt), wpart(k)) for k in range(3)]
                + [pl.BlockSpec((1, dt), wpart(k)) for k in range(3)]
                + [pl.BlockSpec((HY_ORDER, seq, dt), lambda j, bi: (0, 0, j), pipeline_mode=once),
                   pl.BlockSpec((HY_ORDER, seq, dt), lambda j, bi: (0, 0, j), pipeline_mode=once),
                   pl.BlockSpec((HY_ORDER, dt), lambda j, bi: (0, j)),
                   pl.BlockSpec((2 * seq, seq), const, pipeline_mode=once),
                   pl.BlockSpec((seq, 2 * seq), const, pipeline_mode=once)])
    return pl.pallas_call(
        functools.partial(_hyena_body, seq=seq, lanes=min(dt, lanes)),
        grid=(nd, b),
        in_specs=in_specs,
        out_specs=pl.BlockSpec((1, seq, dt), lambda j, bi: (bi, 0, j)),
        out_shape=jax.ShapeDtypeStruct((b, seq, D_HY), BF16),
        compiler_params=_params(("parallel", "parallel")),
        name="hyena",
    )(hy, hy, hy, cw, cw, cw, cb, cb, cb, kr, ki, skip, fwd, inv)


def _layer_norm(x, g, b):
    mu = jnp.mean(x, axis=-1, keepdims=True)
    xc = x - mu
    var = jnp.mean(xc * xc, axis=-1, keepdims=True)
    return xc * lax.rsqrt(var + LN_EPS) * g + b


def _merge_body(*refs, has_pos):
    if has_pos:
        (x_ref, pos_ref, mod_ref, yr_ref, gg_ref, z_ref, ga_ref, gb_ref, wa_ref, wb_ref, wo_ref,
         lg_ref, lb_ref, wq_ref, x1_ref, h2_ref, q_ref) = refs
    else:
        (x_ref, mod_ref, yr_ref, gg_ref, z_ref, ga_ref, gb_ref, wa_ref, wb_ref, wo_ref,
         lg_ref, lb_ref, wq_ref, x1_ref, h2_ref, q_ref) = refs
    d = D_MODEL
    mod = mod_ref[0]
    g1 = mod[:, 2 * d:3 * d]
    sh2 = mod[:, 3 * d:4 * d]
    sc2 = mod[:, 4 * d:5 * d]
    ya = _dot((yr_ref[...].astype(BF16) * gg_ref[0]), wa_ref[...])
    yb = _dot(z_ref[0], wb_ref[...])
    m = ga_ref[0].astype(F32) * ya + gb_ref[0].astype(F32) * yb
    y = _dot(m.astype(BF16), wo_ref[...])
    x = x_ref[0]
    if has_pos:
        x = x + pos_ref[...]
    x1 = _layer_norm(DN_ALPHA * x + g1 * y, lg_ref[...], lb_ref[...])
    x1_ref[0] = x1
    h2 = (x1 * (1.0 + sc2) + sh2).astype(BF16)
    h2_ref[0] = h2
    q_ref[0] = _dot(h2, wq_ref[...]).astype(BF16)


def _merge(x, pos, mod3, mod_row_of_batch, yr2, gg, z, ga, gb, wa, wb, wo, lg, lb, wq, tl):
    b, l, d = x.shape
    nq = wq.shape[1]
    has_pos = pos is not None
    tok = lambda bi, ti: (bi, ti, 0)
    const = lambda bi, ti: (0, 0)
    in_specs = [pl.BlockSpec((1, tl, d), tok)]
    args = [x]
    if has_pos:
        in_specs.append(pl.BlockSpec((tl, d), lambda bi, ti: (ti, 0)))
        args.append(pos)
    in_specs += [pl.BlockSpec((1, 1, 6 * d), lambda bi, ti: (mod_row_of_batch(bi), 0, 0)),
                 pl.BlockSpec((tl, D_RNN), lambda bi, ti: (ti, bi)),
                 pl.BlockSpec((1, tl, d), tok), pl.BlockSpec((1, tl, d), tok),
                 pl.BlockSpec((1, tl, d), tok), pl.BlockSpec((1, tl, d), tok),
                 pl.BlockSpec((d, d), const), pl.BlockSpec((d, d), const), pl.BlockSpec((d, d), const),
                 pl.BlockSpec((1, d), const), pl.BlockSpec((1, d), const),
                 pl.BlockSpec((d, nq), const)]
    args += [mod3, yr2, gg, z, ga, gb, wa, wb, wo, lg, lb, wq]
    return pl.pallas_call(
        functools.partial(_merge_body, has_pos=has_pos),
        grid=(b, l // tl),
        in_specs=in_specs,
        out_specs=[pl.BlockSpec((1, tl, d), tok), pl.BlockSpec((1, tl, d), tok),
                   pl.BlockSpec((1, tl, nq), tok)],
        out_shape=[jax.ShapeDtypeStruct((b, l, d), F32), jax.ShapeDtypeStruct((b, l, d), BF16),
                   jax.ShapeDtypeStruct((b, l, nq), BF16)],
        compiler_params=_params(("parallel", "parallel")),
        name="merge",
    )(*args)


RANK_NONE = 1.0e9


def _topk_rank(problems, k, exact, want_rank):
    n_prob = len(problems)
    s = list(problems)
    ranks = [jnp.full(p.shape, RANK_NONE, F32) if w else None for p, w in zip(problems, want_rank)]
    vals = [[] for _ in range(n_prob)]
    for r in range(k):
        for i in range(n_prob):
            n = s[i].shape[0]
            m = jnp.max(s[i], axis=0, keepdims=True)
            hit = s[i] == m
            if exact:
                iota = lax.broadcasted_iota(jnp.int32, s[i].shape, 0).astype(F32)
                idx = jnp.min(jnp.where(hit, iota, float(n)), axis=0, keepdims=True)
                hit = iota == idx
            if want_rank[i]:
                ranks[i] = jnp.where(hit, float(r), ranks[i])
            s[i] = jnp.where(hit, NEG_INF, s[i])
            vals[i].append(m)
    taken = [jnp.where(a != b, 1.0, 0.0) for a, b in zip(s, problems)]
    if exact:
        ties = [jnp.zeros((1, p.shape[1]), F32) for p in problems]
    else:
        ties = [jnp.where(jnp.sum(t, axis=0, keepdims=True) > float(k), 1.0, 0.0) for t in taken]
    return vals, ranks, taken, ties


def _select_tiles(s1, s2, exact):
    k = PEER_TOPK
    nt = len(s1)
    vals, ranks, _, ties = _topk_rank(s1 + s2, k, exact, [exact] * nt + [True] * nt)
    v1, v2 = vals[:nt], vals[nt:]
    rank1, rank2 = ranks[:nt], ranks[nt:]
    cands = []
    for t in range(nt):
        v2_16 = jnp.concatenate(v2[t], axis=0)
        v2_8 = v2_16[0:8]
        v1_hi = jnp.concatenate(v1[t][8:16], axis=0)
        cands.append(jnp.concatenate([v1[t][0] + v2_16] + [v1[t][a] + v2_8 for a in range(1, 8)]
                                     + [v1_hi + v2[t][0]], axis=0))
    _, _, sels, tcs = _topk_rank(cands, k, exact, [False] * nt)
    outs = []
    for t in range(nt):
        sel, cand = sels[t], cands[t]
        z = jnp.sum(sel * jnp.exp(cand - cand[0:1]), axis=0, keepdims=True)
        counts = [jnp.sum(sel[0:16], axis=0, keepdims=True)]
        counts += [jnp.sum(sel[8 + 8 * a:16 + 8 * a], axis=0, keepdims=True) for a in range(1, 8)]
        counts += [sel[72 + a:73 + a] for a in range(8)]
        ni = jnp.zeros(s1[t].shape, F32)
        for a in range(k):
            is_a = (rank1[t] == float(a)) if exact else (s1[t] == v1[t][a])
            ni = jnp.where(is_a, counts[a], ni)
        e1n = jnp.exp(s1[t] - v1[t][0]) / z
        e2 = jnp.exp(s2[t] - v2[t][0])
        outs.append((e1n, ni, e2.astype(BF16), rank2[t].astype(BF16), ties[t] + ties[nt + t] + tcs[t]))
    return outs


def _peersel_body(q_ref, k_ref, e1n_ref, ni_ref, e2_ref, r2_ref, s1_ref, s2_ref):
    nt = (((1,), (1,)), ((), ()))
    s1_ref[...] = lax.dot_general(k_ref[0], q_ref[:, 0:PEER_HALF], nt, preferred_element_type=F32)
    s2_ref[...] = lax.dot_general(k_ref[1], q_ref[:, PEER_HALF:2 * PEER_HALF], nt, preferred_element_type=F32)
    lane = 128
    n_tiles = s1_ref.shape[1] // lane

    def run(tiles, exact):
        outs = _select_tiles([s1_ref[:, c] for c in tiles], [s2_ref[:, c] for c in tiles], exact)
        for cols, (e1n, ni, e2, r2, _) in zip(tiles, outs):
            e1n_ref[0, :, cols] = e1n
            ni_ref[0, :, cols] = ni
            e2_ref[0, :, cols] = e2
            r2_ref[0, :, cols] = r2
        return [o[4] for o in outs]

    tiles = [slice(lt * lane, (lt + 1) * lane) for lt in range(n_tiles)]
    ties = run(tiles, exact=False)
    for cols, t in zip(tiles, ties):
        @pl.when(jnp.max(t) > 0.0)
        def _(cols=cols):
            run([cols], exact=True)


def _peersel(q2, keys_bf, tt):
    t = q2.shape[0]
    out_spec = pl.BlockSpec((1, N_KEYS, tt), lambda ti, h: (h, 0, ti))
    return pl.pallas_call(
        _peersel_body,
        grid=(t // tt, PEER_HEADS),
        in_specs=[pl.BlockSpec((tt, 2 * PEER_HALF), lambda ti, h: (ti, h)),
                  pl.BlockSpec((2, N_KEYS, PEER_HALF), lambda ti, h: (0, 0, 0))],
        out_specs=[out_spec] * 4,
        out_shape=[jax.ShapeDtypeStruct((PEER_HEADS, N_KEYS, t), F32)] * 2
        + [jax.ShapeDtypeStruct((PEER_HEADS, N_KEYS, t), BF16)] * 2,
        scratch_shapes=[pltpu.VMEM((N_KEYS, tt), F32), pltpu.VMEM((N_KEYS, tt), F32)],
        compiler_params=_params(("parallel", "parallel"), VMEM_LIMIT_MIB - VMEM_RESERVE_MIB),
        name="peersel",
    )(q2, keys_bf)


PEER_EB = 2048
PEER_EG = 512


def _gelu_sigmoid(x):
    k1 = 2.0 * math.sqrt(2.0 / math.pi) * math.log2(math.e)
    k2 = k1 * 0.044715
    z = x * (-k1 - k2 * (x * x))
    return x * (1.0 / (1.0 + jnp.exp2(z)))


def _peermix_body(h2_ref, u_ref, vt_ref, e1n_ref, ni_ref, e2_ref, r2_ref, x1_ref, mod_ref, lg_ref, lb_ref,
                  o_ref, acc_ref, w_ref, act_ref, g_ref, h2t_ref):
    e = pl.program_id(1)
    tt = acc_ref.shape[1]
    n_groups = PEER_EB // PEER_EG
    slabs_per_group = PEER_EG // N_KEYS
    pack = 16
    n_chunks = N_KEYS // pack
    lanes = 256
    zero = jnp.zeros((), BF16)

    @pl.when(e == 0)
    def _():
        acc_ref[...] = jnp.zeros(acc_ref.shape, F32)
        h2t_ref[...] = h2_ref[...].T

    def selection_weights(s):
        for lh in range(tt // lanes):
            cols = slice(lh * lanes, (lh + 1) * lanes)
            g = [None] * n_chunks
            for h in range(PEER_HEADS):
                nrow = jnp.broadcast_to(ni_ref[h, s:s + 1, cols], (pack, lanes)).astype(BF16)
                erow = jnp.broadcast_to(e1n_ref[h, s:s + 1, cols], (pack, lanes)).astype(BF16)
                for c in range(n_chunks):
                    rows = slice(c * pack, (c + 1) * pack)
                    term = jnp.where(r2_ref[h, rows, cols] < nrow, e2_ref[h, rows, cols], zero) * erow
                    g[c] = term if g[c] is None else g[c] + term
            for c in range(n_chunks):
                g_ref[s * N_KEYS + c * pack:s * N_KEYS + (c + 1) * pack, cols] = g[c]

    for p in range(n_groups):
        grp = slice(p * PEER_EG, (p + 1) * PEER_EG)
        act_ref[grp, :] = _dot(u_ref[grp, :], h2t_ref[...])
        for s in range(p * slabs_per_group, (p + 1) * slabs_per_group):
            selection_weights(s)
    for p in range(n_groups):
        grp = slice(p * PEER_EG, (p + 1) * PEER_EG)
        for c in range(PEER_EG // pack):
            rows = slice(p * PEER_EG + c * pack, p * PEER_EG + (c + 1) * pack)
            w_ref[rows, :] = _gelu_sigmoid(act_ref[rows, :]).astype(BF16) * g_ref[rows, :]
        acc_ref[...] += _dot(vt_ref[:, grp], w_ref[grp, :])

    @pl.when(e == pl.num_programs(1) - 1)
    def _():
        d = D_MODEL
        g2 = mod_ref[0][:, 5 * d:6 * d]
        y = acc_ref[...].T
        o_ref[...] = _layer_norm(DN_ALPHA * x1_ref[...] + g2 * y, lg_ref[...], lb_ref[...])


def _peermix(h2, u_bf, vt_bf, e1n, ni, e2, r2, x1, mod3, mod_row_of_tile, lg, lb, tt):
    t, d = h2.shape
    nb = u_bf.shape[0] // PEER_EB
    ns = PEER_EB // N_KEYS
    tile = lambda ti, e: (ti, 0)
    const = lambda ti, e: (0, 0)
    return pl.pallas_call(
        _peermix_body,
        grid=(t // tt, nb),
        in_specs=[pl.BlockSpec((tt, d), tile),
                  pl.BlockSpec((PEER_EB, d), lambda ti, e: (e, 0)),
                  pl.BlockSpec((d, PEER_EB), lambda ti, e: (0, e)),
                  pl.BlockSpec((PEER_HEADS, ns, tt), lambda ti, e: (0, e, ti)),
                  pl.BlockSpec((PEER_HEADS, ns, tt), lambda ti, e: (0, e, ti)),
                  pl.BlockSpec((PEER_HEADS, N_KEYS, tt), lambda ti, e: (0, 0, ti)),
                  pl.BlockSpec((PEER_HEADS, N_KEYS, tt), lambda ti, e: (0, 0, ti)),
                  pl.BlockSpec((tt, d), tile),
                  pl.BlockSpec((1, 1, 6 * d), lambda ti, e: (mod_row_of_tile(ti), 0, 0)),
                  pl.BlockSpec((1, d), const), pl.BlockSpec((1, d), const)],
        out_specs=pl.BlockSpec((tt, d), tile),
        out_shape=jax.ShapeDtypeStruct((t, d), F32),
        scratch_shapes=[pltpu.VMEM((d, tt), F32), pltpu.VMEM((PEER_EB, tt), BF16),
                        pltpu.VMEM((PEER_EB, tt), F32), pltpu.VMEM((PEER_EB, tt), BF16),
                        pltpu.VMEM((d, tt), BF16)],
        compiler_params=_params(("parallel", "arbitrary")),
        name="peermix",
    )(h2, u_bf, vt_bf, e1n, ni, e2, r2, x1, mod3, lg, lb)


def _prep_params(p):
    hd = RNN_HEAD_DIM
    q = dict(p)
    q["w_in_bf"] = p["w_in"].astype(BF16)
    q["wg"] = jnp.transpose(p["rnn_gate_w"], (2, 3, 0, 1, 4)).reshape(N_RNN_HEADS, hd, 4 * hd).astype(BF16)
    q["bg"] = jnp.transpose(p["rnn_gate_b"].reshape(2, 2, N_RNN_HEADS, hd), (2, 0, 1, 3)).reshape(
        N_RNN_HEADS, 1, 4 * hd)
    q["lam"] = jnp.transpose(p["rnn_lambda"].reshape(2, N_RNN_HEADS, hd), (1, 0, 2))
    q["rnn_cb"] = p["rnn_conv_b"][None, :]
    q["hy_cb"] = p["hy_conv_b"][None, :]
    q["wa"] = p["w_branch_a"].astype(BF16)
    q["wb"] = p["w_branch_b"].astype(BF16)
    q["wo"] = p["w_out"].astype(BF16)
    q["wq"] = p["peer_w_query"].astype(BF16)
    q["ln1_g2"] = p["ln1_g"][None, :]
    q["ln1_b2"] = p["ln1_b"][None, :]
    q["ln2_g2"] = p["ln2_g"][None, :]
    q["ln2_b2"] = p["ln2_b"][None, :]
    return q


def _mixer_group(x, pos, mod3, mod_row_of_batch, h0, q, tl, bb, dt, hy_lanes):
    b, l, _ = x.shape
    fwd, inv = _dft_matrices(l)
    kr, ki = _hyfilt(l, fwd, q)
    rx2, gg, hy, ga, gb = _inproj(x, pos, mod3, q["w_in_bf"], mod_row_of_batch, tl)
    y3, st = _rglru(rx2.reshape(l, b, D_RNN), h0, q["wg"], q["bg"], q["lam"], q["rnn_conv_w"],
                    q["rnn_cb"], bb)
    z = _hyena(hy, q["hy_conv_w"], q["hy_cb"], kr, ki, q["hy_skip"], fwd, inv, dt, hy_lanes)
    x1, h2, qq = _merge(x, pos, mod3, mod_row_of_batch, y3.reshape(l, b * D_RNN), gg, z, ga, gb,
                        q["wa"], q["wb"], q["wo"], q["ln1_g2"], q["ln1_b2"], q["wq"], tl)
    return x1, h2, qq, st


def _value_table_t(v):
    return v.astype(BF16).T


def _peer_group(x1, h2, qq, mod3, mod_row_of_tile, q, tt_sel, tt_mix):
    b, l, d = x1.shape
    t = b * l
    e1n, ni, e2, r2 = _peersel(qq.reshape(t, qq.shape[-1]), q["keys_bf"], tt_sel)
    out = _peermix(h2.reshape(t, d), q["u_bf"], q["vt_bf"], e1n, ni, e2, r2, x1.reshape(t, d), mod3,
                   mod_row_of_tile, q["ln2_g2"], q["ln2_b2"], tt_mix)
    return out.reshape(b, l, d)


def _grid_pos_embed(n_tokens):
    rows = n_tokens // GRID_W
    t = np.arange(rows * GRID_W)
    r = (t // GRID_W).astype(np.float32)
    col = (t % GRID_W).astype(np.float32)
    quarter = D_MODEL // 4
    omega = (1.0 / (10000.0 ** (np.arange(quarter, dtype=np.float32) / np.float32(quarter)))).astype(np.float32)
    er = r[:, None] * omega[None, :]
    ec = col[:, None] * omega[None, :]
    return jnp.asarray(np.concatenate([np.sin(er), np.cos(er), np.sin(ec), np.cos(ec)], axis=-1), dtype=F32)


def _plan_tiles(batch, seq):
    tt = 2 * V7X_MXU_DIM
    tl = min(seq, tt)
    bb = min(batch, 4 * 8)
    dt = D_HY if seq * D_HY * 4 <= MIB else D_HY // 2
    hy_lanes = min(dt, 2 * V7X_MXU_DIM) if seq * D_HY * 4 <= MIB else V7X_MXU_DIM
    assert seq % tl == 0 and batch % bb == 0 and (batch * seq) % tt == 0 and (seq % tt == 0 or tt % seq == 0)
    return dict(tl=tl, bb=bb, dt=dt, hy_lanes=hy_lanes), tt


PARAM_NAMES = ("w_ada", "b_ada", "w_in", "rnn_conv_w", "rnn_conv_b", "rnn_gate_w", "rnn_gate_b", "rnn_lambda",
               "hy_conv_w", "hy_conv_b", "hy_ffn_w1", "hy_ffn_b1", "hy_ffn_w2", "hy_ffn_b2", "hy_ffn_w3",
               "hy_ffn_b3", "hy_sin_freq", "hy_skip", "w_branch_a", "w_branch_b", "w_out", "ln1_g", "ln1_b",
               "ln2_g", "ln2_b", "peer_w_query", "peer_sub_keys", "peer_u", "peer_v")


def kernel(x_prompt, x_sample, state_rglru, c, c_ctx, w_ada, b_ada, w_in, rnn_conv_w, rnn_conv_b, rnn_gate_w,
           rnn_gate_b, rnn_lambda, hy_conv_w, hy_conv_b, hy_ffn_w1, hy_ffn_b1, hy_ffn_w2, hy_ffn_b2, hy_ffn_w3,
           hy_ffn_b3, hy_sin_freq, hy_skip, w_branch_a, w_branch_b, w_out, ln1_g, ln1_b, ln2_g, ln2_b,
           peer_w_query, peer_sub_keys, peer_u, peer_v):
    stacked = dict(zip(PARAM_NAMES, (w_ada, b_ada, w_in, rnn_conv_w, rnn_conv_b, rnn_gate_w, rnn_gate_b,
                                     rnn_lambda, hy_conv_w, hy_conv_b, hy_ffn_w1, hy_ffn_b1, hy_ffn_w2,
                                     hy_ffn_b2, hy_ffn_w3, hy_ffn_b3, hy_sin_freq, hy_skip, w_branch_a,
                                     w_branch_b, w_out, ln1_g, ln1_b, ln2_g, ln2_b, peer_w_query,
                                     peer_sub_keys, peer_u, peer_v)))
    depth = w_ada.shape[0]
    bp, lp, d = x_prompt.shape
    bs, ls, _ = x_sample.shape
    assert bs + 1 <= MOD_ROWS
    cond = jnp.zeros((MOD_ROWS, d), F32).at[0].set(c_ctx).at[1:1 + bs].set(c)
    pos = _grid_pos_embed(ls)
    prompt_tiles, tt = _plan_tiles(bp, lp)
    sample_tiles, tt_sample = _plan_tiles(bs, ls)
    assert tt == tt_sample and ls % tt == 0
    sample_tiles_per_batch = ls // tt

    xp, xs = x_prompt, x_sample
    ctx_states = []
    for layer in range(depth):
        q = _prep_params({name: w[layer] for name, w in stacked.items()})
        q["keys_bf"] = q["peer_sub_keys"].astype(BF16)
        q["u_bf"] = q["peer_u"].astype(BF16)
        q["vt_bf"] = _value_table_t(q["peer_v"])
        mod3 = _ada(cond, q["w_ada"], q["b_ada"][None, :]).reshape(MOD_ROWS, 1, 6 * d)

        x1, h2, qq, st = _mixer_group(xp, None, mod3, lambda bi: 0, jnp.zeros((2, bp, D_RNN), F32), q,
                                      **prompt_tiles)
        xp = _peer_group(x1, h2, qq, mod3, lambda ti: 0, q, tt, tt)
        ctx_states.append(jnp.transpose(st, (1, 0, 2)))

        pos_l = pos if layer == 0 else None
        h0 = jnp.transpose(state_rglru[:, layer], (1, 0, 2))
        x1, h2, qq, _ = _mixer_group(xs, pos_l, mod3, lambda bi: bi + 1, h0, q, **sample_tiles)
        xs = _peer_group(x1, h2, qq, mod3, lambda ti: 1 + ti // sample_tiles_per_batch, q, tt, tt)

    new_state = jnp.stack(ctx_states, axis=1).astype(x_prompt.dtype)
    return (xp, xs, new_state)
```

```python
import functools
import math

import numpy as np
import jax
import jax.numpy as jnp
from jax import lax
from jax.experimental import pallas as pl
from jax.experimental.pallas import tpu as pltpu

F32 = jnp.float32
BF16 = jnp.bfloat16

D_MODEL = 1024
D_RNN = 1024
N_RNN_HEADS = 4
RNN_HEAD_DIM = D_RNN // N_RNN_HEADS
RNN_CONV_W = 4
RGLRU_C = 8.0
D_HY = 1024
HY_ORDER = 2
HY_EMB_BANDS = 16
HY_EMB_DIM = 1 + 2 * HY_EMB_BANDS
HY_EMB_PAD = 64
HY_FILTER_HIDDEN = 64
HY_DECAY_TARGET = 1e-2
HY_MIN_DECAY = math.log(HY_DECAY_TARGET) / 1.5
HY_MAX_DECAY = math.log(HY_DECAY_TARGET) / 0.3
GRID_W = 64
N_KEYS = 128
N_EXPERTS = N_KEYS * N_KEYS
PEER_HEADS = 8
PEER_HALF = 128
PEER_TOPK = 16
DEPTH = 1
DN_ALPHA = (2.0 * DEPTH) ** 0.25
LN_EPS = 1e-5

MOD_ROWS = 16
MIB = 1024 * 1024
NEG_INF = float("-inf")

V7X_VMEM_MIB = 64
VMEM_RESERVE_MIB = 8
VMEM_LIMIT_MIB = V7X_VMEM_MIB - VMEM_RESERVE_MIB
V7X_LANES = 128
V7X_MXU_DIM = 256


def _params(semantics, vmem_mib=VMEM_LIMIT_MIB):
    return pltpu.CompilerParams(dimension_semantics=semantics, vmem_limit_bytes=vmem_mib * MIB)


def _gelu(x):
    return jax.nn.gelu(x, approximate=True)


def _dot(a, b):
    return jnp.dot(a, b, preferred_element_type=F32)


def _dot_f32(a, b):
    return jnp.dot(a, b, preferred_element_type=F32, precision=lax.Precision.HIGHEST)


def _ada_body(c_ref, w_ref, b_ref, o_ref):
    c = c_ref[...]
    o_ref[...] = _dot_f32(c * jax.nn.sigmoid(c), w_ref[...]) + b_ref[...]


def _ada(cond, w_ada, b_ada):
    n = w_ada.shape[1]
    tn = 1024
    return pl.pallas_call(
        _ada_body,
        grid=(n // tn,),
        in_specs=[pl.BlockSpec((MOD_ROWS, D_MODEL), lambda j: (0, 0)),
                  pl.BlockSpec((D_MODEL, tn), lambda j: (0, j)),
                  pl.BlockSpec((1, tn), lambda j: (0, j))],
        out_specs=pl.BlockSpec((MOD_ROWS, tn), lambda j: (0, j)),
        out_shape=jax.ShapeDtypeStruct((MOD_ROWS, n), F32),
        compiler_params=_params(("arbitrary",), V7X_VMEM_MIB // 2),
        name="ada",
    )(cond, w_ada, b_ada)


def _inproj_body(*refs, has_pos):
    if has_pos:
        x_ref, pos_ref, mod_ref, w_ref, rx_ref, gg_ref, hy_ref, ga_ref, gb_ref = refs
    else:
        x_ref, mod_ref, w_ref, rx_ref, gg_ref, hy_ref, ga_ref, gb_ref = refs
    x = x_ref[0]
    if has_pos:
        x = x + pos_ref[...]
    mod = mod_ref[0]
    sh1 = mod[:, 0:D_MODEL]
    sc1 = mod[:, D_MODEL:2 * D_MODEL]
    h = (x * (1.0 + sc1) + sh1).astype(BF16)
    o = 0
    rx_ref[...] = _dot(h, w_ref[:, o:o + D_RNN])
    o += D_RNN
    gg_ref[0] = _gelu(_dot(h, w_ref[:, o:o + D_RNN])).astype(BF16)
    o += D_RNN
    for j in range(3):
        hy_ref[0, :, j * D_HY:(j + 1) * D_HY] = _dot(h, w_ref[:, o:o + D_HY])
        o += D_HY
    ga_ref[0] = jax.nn.sigmoid(_dot(h, w_ref[:, o:o + D_MODEL])).astype(BF16)
    o += D_MODEL
    gb_ref[0] = jax.nn.sigmoid(_dot(h, w_ref[:, o:o + D_MODEL])).astype(BF16)


def _inproj(x, pos, mod3, w_in_bf, mod_row_of_batch, tl):
    b, l, _ = x.shape
    d_in = w_in_bf.shape[1]
    has_pos = pos is not None
    in_specs = [pl.BlockSpec((1, tl, D_MODEL), lambda bi, ti: (bi, ti, 0))]
    args = [x]
    if has_pos:
        in_specs.append(pl.BlockSpec((tl, D_MODEL), lambda bi, ti: (ti, 0)))
        args.append(pos)
    in_specs += [pl.BlockSpec((1, 1, 6 * D_MODEL), lambda bi, ti: (mod_row_of_batch(bi), 0, 0)),
                 pl.BlockSpec((D_MODEL, d_in), lambda bi, ti: (0, 0), pipeline_mode=pl.Buffered(1))]
    args += [mod3, w_in_bf]
    tok = lambda bi, ti: (bi, ti, 0)
    out_specs = [pl.BlockSpec((tl, D_RNN), lambda bi, ti: (ti, bi)),
                 pl.BlockSpec((1, tl, D_RNN), tok),
                 pl.BlockSpec((1, tl, 3 * D_HY), tok),
                 pl.BlockSpec((1, tl, D_MODEL), tok),
                 pl.BlockSpec((1, tl, D_MODEL), tok)]
    out_shape = [jax.ShapeDtypeStruct((l, b * D_RNN), F32),
                 jax.ShapeDtypeStruct((b, l, D_RNN), BF16),
                 jax.ShapeDtypeStruct((b, l, 3 * D_HY), F32),
                 jax.ShapeDtypeStruct((b, l, D_MODEL), BF16),
                 jax.ShapeDtypeStruct((b, l, D_MODEL), BF16)]
    return pl.pallas_call(
        functools.partial(_inproj_body, has_pos=has_pos),
        grid=(b, l // tl),
        in_specs=in_specs, out_specs=out_specs, out_shape=out_shape,
        compiler_params=_params(("parallel", "parallel")),
        name="inproj",
    )(*args)


def _softplus(x):
    return jnp.maximum(x, 0.0) + jnp.log(1.0 + jnp.exp(-jnp.abs(x)))


def _rglru_body(rx_ref, h0_ref, wg_ref, bg_ref, lam_ref, cw_ref, cb_ref, y_ref, st_ref, a_s, u_s,
                *, seq, bb, tc):
    hd = RNN_HEAD_DIM
    nchunks = seq // tc
    cw = cw_ref[...]
    cb = cb_ref[...]
    for d in range(2):
        coef = -RGLRU_C * _softplus(-lam_ref[0, d:d + 1, :])
        wd = wg_ref[0, :, 2 * d * hd:2 * (d + 1) * hd]
        bd = bg_ref[0, :, 2 * d * hd:2 * (d + 1) * hd]

        def chunk(ci, h, d=d, coef=coef, wd=wd, bd=bd):
            c = ci if d == 0 else nchunks - 1 - ci
            t0 = pl.multiple_of(c * tc, tc)
            lo_ok = jnp.where(c > 0, 1.0, 0.0)
            hi_ok = jnp.where(c < nchunks - 1, 1.0, 0.0)
            lo = rx_ref[pl.ds(jnp.maximum(t0 - 2, 0), 2)] * lo_ok
            hi = rx_ref[pl.ds(jnp.minimum(t0 + tc, seq - 1), 1)] * hi_ok
            xe = jnp.concatenate([lo, rx_ref[pl.ds(t0, tc)], hi], axis=0)
            xc = cb[None] + sum(cw[k:k + 1][None] * xe[k:k + tc] for k in range(RNN_CONV_W))
            xc2 = xc.reshape(tc * bb, hd)
            g = _dot(xc2.astype(BF16), wd) + bd
            r = jax.nn.sigmoid(g[:, :hd])
            i = jax.nn.sigmoid(g[:, hd:])
            a = jnp.exp(coef * r)
            u = jnp.sqrt(1.0 - a * a) * (i * xc2)
            a_s[...] = a.reshape(tc, bb, hd)
            u_s[...] = u.reshape(tc, bb, hd)

            def step(j, h):
                tau = j if d == 0 else tc - 1 - j
                h = a_s[tau] * h + u_s[tau]
                if d == 0:
                    y_ref[t0 + tau] = h
                else:
                    y_ref[t0 + tau] = y_ref[t0 + tau] + h
                return h

            return lax.fori_loop(0, tc, step, h, unroll=8)

        h_fin = lax.fori_loop(0, nchunks, chunk, h0_ref[d])
        st_ref[d] = h_fin


def _rglru(rx3, h0, wg, bg, lam, cw, cb, bb):
    seq, b, _ = rx3.shape
    hd = RNN_HEAD_DIM
    tc = max(8, 512 // bb)
    blk = lambda bi, hi: (0, bi, hi)
    return pl.pallas_call(
        functools.partial(_rglru_body, seq=seq, bb=bb, tc=tc),
        grid=(b // bb, N_RNN_HEADS),
        in_specs=[pl.BlockSpec((seq, bb, hd), blk),
                  pl.BlockSpec((2, bb, hd), blk),
                  pl.BlockSpec((1, hd, 4 * hd), lambda bi, hi: (hi, 0, 0)),
                  pl.BlockSpec((1, 1, 4 * hd), lambda bi, hi: (hi, 0, 0)),
                  pl.BlockSpec((1, 2, hd), lambda bi, hi: (hi, 0, 0)),
                  pl.BlockSpec((RNN_CONV_W, hd), lambda bi, hi: (0, hi)),
                  pl.BlockSpec((1, hd), lambda bi, hi: (0, hi))],
        out_specs=[pl.BlockSpec((seq, bb, hd), blk),
                   pl.BlockSpec((2, bb, hd), blk)],
        out_shape=[jax.ShapeDtypeStruct((seq, b, D_RNN), F32),
                   jax.ShapeDtypeStruct((2, b, D_RNN), F32)],
        scratch_shapes=[pltpu.VMEM((tc, bb, hd), F32), pltpu.VMEM((tc, bb, hd), F32)],
        compiler_params=_params(("parallel", "parallel")),
        name="rglru",
    )(rx3, h0, wg, bg, lam, cw, cb)


def _dft_matrices(seq):
    k = np.arange(seq, dtype=np.float64)[:, None]
    s = np.arange(seq, dtype=np.float64)[None, :]
    ang = np.pi * k * s / seq
    top = np.cos(ang)
    bot = -np.sin(ang)
    bot[0, :] = np.where(np.arange(seq) % 2 == 0, 1.0, -1.0)
    fwd = np.concatenate([top, bot], axis=0)
    return jnp.asarray(fwd, dtype=BF16), jnp.asarray(fwd.T, dtype=BF16)


def _filter_features(seq):
    t = np.arange(seq, dtype=np.float32)
    t_norm = t / np.float32(max(seq - 1, 1))
    w = (np.float32(2.0 * math.pi) * t / np.float32(seq)).astype(np.float32)
    bands = np.linspace(1e-4, HY_EMB_BANDS - 1, HY_EMB_BANDS, dtype=np.float32)
    fw = w[:, None] * bands[None, :]
    z = np.concatenate([t_norm[:, None], np.cos(fw), -np.sin(fw)], axis=-1).astype(np.float32)
    zp = np.zeros((seq, HY_EMB_PAD), np.float32)
    zp[:, :HY_EMB_DIM] = z
    deltas = np.abs(np.linspace(HY_MIN_DECAY, HY_MAX_DECAY, D_HY, dtype=np.float32))
    return jnp.asarray(zp), jnp.asarray(t_norm[:, None]), jnp.asarray(deltas[None, :])


def _dot_split(f_bf, x):
    hi = x.astype(BF16)
    lo = (x - hi.astype(F32)).astype(BF16)
    return _dot(f_bf, hi) + _dot(f_bf, lo)


def _hyfilt_body(z_ref, tn_ref, dl_ref, w1_ref, b1_ref, w2_ref, b2_ref, fr_ref, w3f_ref, b3f_ref,
                 w3b_ref, b3b_ref, f_ref, kr_ref, ki_ref, *, seq):
    freq = fr_ref[...]
    hid = jnp.sin(freq * (_dot_f32(z_ref[...], w1_ref[...]) + b1_ref[...]))
    hid = jnp.sin(freq * (_dot_f32(hid, w2_ref[...]) + b2_ref[...]))
    decay = jnp.exp(-tn_ref[...] * dl_ref[...])
    ff = (_dot_f32(hid, w3f_ref[...]) + b3f_ref[...]) * decay
    fb = (_dot_f32(hid, w3b_ref[...]) + b3b_ref[...]) * decay
    row = lax.broadcasted_iota(jnp.int32, ff.shape, 0)
    fb = jnp.where(row == 0, 0.0, fb)
    s = ff + fb
    dm = ff - fb
    sign = jnp.where((row & 1) == 0, 1.0, -1.0)
    nyq = jnp.sum(s * sign, axis=0, keepdims=True)
    kr = _dot_split(f_ref[0:seq, :], s)
    ki = _dot_split(f_ref[seq:2 * seq, :], dm)
    scale = jnp.where(row == 0, 0.5 / seq, 1.0 / seq)
    kr_ref[0] = kr * scale
    ki_ref[0] = jnp.where(row == 0, nyq, ki) * scale


def _hyfilt(seq, fwd, p):
    z, tn, dl = _filter_features(seq)
    dt = 512
    nd = D_HY // dt
    hidn = HY_FILTER_HIDDEN
    const = lambda o, j: (0, 0)
    w3 = p["hy_ffn_w3"]
    b3 = p["hy_ffn_b3"][None, :]
    in_specs = [pl.BlockSpec((seq, HY_EMB_PAD), const),
                pl.BlockSpec((seq, 1), const),
                pl.BlockSpec((1, dt), lambda o, j: (0, j)),
                pl.BlockSpec((HY_EMB_PAD, hidn), const),
                pl.BlockSpec((1, hidn), const),
                pl.BlockSpec((hidn, hidn), const),
                pl.BlockSpec((1, hidn), const),
                pl.BlockSpec((1, hidn), const),
                pl.BlockSpec((hidn, dt), lambda o, j: (0, o * nd + j)),
                pl.BlockSpec((1, dt), lambda o, j: (0, o * nd + j)),
                pl.BlockSpec((hidn, dt), lambda o, j: (0, (HY_ORDER + o) * nd + j)),
                pl.BlockSpec((1, dt), lambda o, j: (0, (HY_ORDER + o) * nd + j)),
                pl.BlockSpec((2 * seq, seq), const)]
    w1p = jnp.zeros((HY_EMB_PAD, hidn), F32).at[:HY_EMB_DIM].set(p["hy_ffn_w1"])
    out_spec = pl.BlockSpec((1, seq, dt), lambda o, j: (o, 0, j))
    return pl.pallas_call(
        functools.partial(_hyfilt_body, seq=seq),
        grid=(HY_ORDER, nd),
        in_specs=in_specs,
        out_specs=[out_spec, out_spec],
        out_shape=[jax.ShapeDtypeStruct((HY_ORDER, seq, D_HY), F32)] * 2,
        compiler_params=_params(("parallel", "parallel"), VMEM_LIMIT_MIB - VMEM_RESERVE_MIB),
        name="hyfilt",
    )(z, tn, dl, w1p, p["hy_ffn_b1"][None, :], p["hy_ffn_w2"], p["hy_ffn_b2"][None, :],
      p["hy_sin_freq"][None, :], w3, b3, w3, b3, fwd)


def _hyena_body(x1_ref, x2_ref, v_ref, w1_ref, w2_ref, wv_ref, b1_ref, b2_ref, bv_ref,
                kr_ref, ki_ref, sk_ref, f_ref, ft_ref, o_ref, *, seq, lanes):
    dt = v_ref.shape[2]
    row = lax.broadcasted_iota(jnp.int32, (seq, lanes), 0)
    first = row == 0
    last = row == seq - 1

    def conv3(x_ref, w_ref, b_ref, cols):
        x = x_ref[0, :, cols]
        xm = jnp.where(first, 0.0, pltpu.roll(x, 1, 0))
        xp = jnp.where(last, 0.0, pltpu.roll(x, seq - 1, 0))
        return w_ref[0:1, cols] * xm + w_ref[1:2, cols] * x + w_ref[2:3, cols] * xp + b_ref[:, cols]

    def spectrum_product(uf, o, cols):
        kr = kr_ref[o, :, cols]
        ki = ki_ref[o, :, cols]
        kiz = jnp.where(first, 0.0, ki)
        krb = jnp.where(first, ki, kr)
        top = uf[:seq]
        bot = uf[seq:]
        return jnp.concatenate([top * kr - bot * kiz, top * kiz + bot * krb], axis=0).astype(BF16)

    chains = [slice(c * lanes, (c + 1) * lanes) for c in range(dt // lanes)]
    u = [conv3(v_ref, wv_ref, bv_ref, cols) for cols in chains]
    for o, gate_ref, gw_ref, gb_ref in ((0, x1_ref, w1_ref, b1_ref), (1, x2_ref, w2_ref, b2_ref)):
        uf = [_dot(f_ref[...], uc.astype(BF16)) for uc in u]
        y = []
        for cols, ufc in zip(chains, uf):
            y.append(_dot(ft_ref[...], spectrum_product(ufc, o, cols)))
        u = [conv3(gate_ref, gw_ref, gb_ref, cols) * (yc + uc * sk_ref[o:o + 1, cols])
             for cols, yc, uc in zip(chains, y, u)]
    for cols, z in zip(chains, u):
        o_ref[0, :, cols] = z.astype(BF16)


def _hyena(hy, cw, cb, kr, ki, skip, fwd, inv, dt, lanes):
    b, seq, _ = hy.shape
    nd = D_HY // dt
    const = lambda j, bi: (0, 0)
    part = lambda k: (lambda j, bi: (bi, 0, k * nd + j))
    wpart = lambda k: (lambda j, bi: (0, k * nd + j))
    once = pl.Buffered(1)
    in_specs = ([pl.BlockSpec((1, seq, dt), part(k)) for k in range(3)]
                + [pl.BlockSpec((3, dt), wpart(k)) for k in range(3)]
                + [pl.BlockSpec((1, dt), wpart(k)) for k in range(3)]
                + [pl.BlockSpec((HY_ORDER, seq, dt), lambda j, bi: (0, 0, j), pipeline_mode=once),
                   pl.BlockSpec((HY_ORDER, seq, dt), lambda j, bi: (0, 0, j), pipeline_mode=once),
                   pl.BlockSpec((HY_ORDER, dt), lambda j, bi: (0, j)),
                   pl.BlockSpec((2 * seq, seq), const, pipeline_mode=once),
                   pl.BlockSpec((seq, 2 * seq), const, pipeline_mode=once)])
    return pl.pallas_call(
        functools.partial(_hyena_body, seq=seq, lanes=min(dt, lanes)),
        grid=(nd, b),
        in_specs=in_specs,
        out_specs=pl.BlockSpec((1, seq, dt), lambda j, bi: (bi, 0, j)),
        out_shape=jax.ShapeDtypeStruct((b, seq, D_HY), BF16),
        compiler_params=_params(("parallel", "parallel")),
        name="hyena",
    )(hy, hy, hy, cw, cw, cw, cb, cb, cb, kr, ki, skip, fwd, inv)


def _layer_norm(x, g, b):
    mu = jnp.mean(x, axis=-1, keepdims=True)
    xc = x - mu
    var = jnp.mean(xc * xc, axis=-1, keepdims=True)
    return xc * lax.rsqrt(var + LN_EPS) * g + b


def _merge_body(*refs, has_pos):
    if has_pos:
        (x_ref, pos_ref, mod_ref, yr_ref, gg_ref, z_ref, ga_ref, gb_ref, wa_ref, wb_ref, wo_ref,
         lg_ref, lb_ref, wq_ref, x1_ref, h2_ref, q_ref) = refs
    else:
        (x_ref, mod_ref, yr_ref, gg_ref, z_ref, ga_ref, gb_ref, wa_ref, wb_ref, wo_ref,
         lg_ref, lb_ref, wq_ref, x1_ref, h2_ref, q_ref) = refs
    d = D_MODEL
    mod = mod_ref[0]
    g1 = mod[:, 2 * d:3 * d]
    sh2 = mod[:, 3 * d:4 * d]
    sc2 = mod[:, 4 * d:5 * d]
    ya = _dot((yr_ref[...].astype(BF16) * gg_ref[0]), wa_ref[...])
    yb = _dot(z_ref[0], wb_ref[...])
    m = ga_ref[0].astype(F32) * ya + gb_ref[0].astype(F32) * yb
    y = _dot(m.astype(BF16), wo_ref[...])
    x = x_ref[0]
    if has_pos:
        x = x + pos_ref[...]
    x1 = _layer_norm(DN_ALPHA * x + g1 * y, lg_ref[...], lb_ref[...])
    x1_ref[0] = x1
    h2 = (x1 * (1.0 + sc2) + sh2).astype(BF16)
    h2_ref[0] = h2
    q_ref[0] = _dot(h2, wq_ref[...]).astype(BF16)


def _merge(x, pos, mod3, mod_row_of_batch, yr2, gg, z, ga, gb, wa, wb, wo, lg, lb, wq, tl):
    b, l, d = x.shape
    nq = wq.shape[1]
    has_pos = pos is not None
    tok = lambda bi, ti: (bi, ti, 0)
    const = lambda bi, ti: (0, 0)
    in_specs = [pl.BlockSpec((1, tl, d), tok)]
    args = [x]
    if has_pos:
        in_specs.append(pl.BlockSpec((tl, d), lambda bi, ti: (ti, 0)))
        args.append(pos)
    in_specs += [pl.BlockSpec((1, 1, 6 * d), lambda bi, ti: (mod_row_of_batch(bi), 0, 0)),
                 pl.BlockSpec((tl, D_RNN), lambda bi, ti: (ti, bi)),
                 pl.BlockSpec((1, tl, d), tok), pl.BlockSpec((1, tl, d), tok),
                 pl.BlockSpec((1, tl, d), tok), pl.BlockSpec((1, tl, d), tok),
                 pl.BlockSpec((d, d), const), pl.BlockSpec((d, d), const), pl.BlockSpec((d, d), const),
                 pl.BlockSpec((1, d), const), pl.BlockSpec((1, d), const),
                 pl.BlockSpec((d, nq), const)]
    args += [mod3, yr2, gg, z, ga, gb, wa, wb, wo, lg, lb, wq]
    return pl.pallas_call(
        functools.partial(_merge_body, has_pos=has_pos),
        grid=(b, l // tl),
        in_specs=in_specs,
        out_specs=[pl.BlockSpec((1, tl, d), tok), pl.BlockSpec((1, tl, d), tok),
                   pl.BlockSpec((1, tl, nq), tok)],
        out_shape=[jax.ShapeDtypeStruct((b, l, d), F32), jax.ShapeDtypeStruct((b, l, d), BF16),
                   jax.ShapeDtypeStruct((b, l, nq), BF16)],
        compiler_params=_params(("parallel", "parallel")),
        name="merge",
    )(*args)


RANK_NONE = 1.0e9


def _topk_rank(problems, k, exact, want_rank):
    n_prob = len(problems)
    s = list(problems)
    ranks = [jnp.full(p.shape, RANK_NONE, F32) if w else None for p, w in zip(problems, want_rank)]
    vals = [[] for _ in range(n_prob)]
    for r in range(k):
        for i in range(n_prob):
            n = s[i].shape[0]
            m = jnp.max(s[i], axis=0, keepdims=True)
            hit = s[i] == m
            if exact:
                iota = lax.broadcasted_iota(jnp.int32, s[i].shape, 0).astype(F32)
                idx = jnp.min(jnp.where(hit, iota, float(n)), axis=0, keepdims=True)
                hit = iota == idx
            if want_rank[i]:
                ranks[i] = jnp.where(hit, float(r), ranks[i])
            s[i] = jnp.where(hit, NEG_INF, s[i])
            vals[i].append(m)
    taken = [jnp.where(a != b, 1.0, 0.0) for a, b in zip(s, problems)]
    if exact:
        ties = [jnp.zeros((1, p.shape[1]), F32) for p in problems]
    else:
        ties = [jnp.where(jnp.sum(t, axis=0, keepdims=True) > float(k), 1.0, 0.0) for t in taken]
    return vals, ranks, taken, ties


def _select_tiles(s1, s2, exact):
    k = PEER_TOPK
    nt = len(s1)
    vals, ranks, _, ties = _topk_rank(s1 + s2, k, exact, [exact] * nt + [True] * nt)
    v1, v2 = vals[:nt], vals[nt:]
    rank1, rank2 = ranks[:nt], ranks[nt:]
    cands = []
    for t in range(nt):
        v2_16 = jnp.concatenate(v2[t], axis=0)
        v2_8 = v2_16[0:8]
        v1_hi = jnp.concatenate(v1[t][8:16], axis=0)
        cands.append(jnp.concatenate([v1[t][0] + v2_16] + [v1[t][a] + v2_8 for a in range(1, 8)]
                                     + [v1_hi + v2[t][0]], axis=0))
    _, _, sels, tcs = _topk_rank(cands, k, exact, [False] * nt)
    outs = []
    for t in range(nt):
        sel, cand = sels[t], cands[t]
        z = jnp.sum(sel * jnp.exp(cand - cand[0:1]), axis=0, keepdims=True)
        counts = [jnp.sum(sel[0:16], axis=0, keepdims=True)]
        counts += [jnp.sum(sel[8 + 8 * a:16 + 8 * a], axis=0, keepdims=True) for a in range(1, 8)]
        counts += [sel[72 + a:73 + a] for a in range(8)]
        ni = jnp.zeros(s1[t].shape, F32)
        for a in range(k):
            is_a = (rank1[t] == float(a)) if exact else (s1[t] == v1[t][a])
            ni = jnp.where(is_a, counts[a], ni)
        e1n = jnp.exp(s1[t] - v1[t][0]) / z
        e2 = jnp.exp(s2[t] - v2[t][0])
        outs.append((e1n, ni, e2.astype(BF16), rank2[t].astype(BF16), ties[t] + ties[nt + t] + tcs[t]))
    return outs


def _peersel_body(q_ref, k_ref, e1n_ref, ni_ref, e2_ref, r2_ref, s1_ref, s2_ref):
    nt = (((1,), (1,)), ((), ()))
    s1_ref[...] = lax.dot_general(k_ref[0], q_ref[:, 0:PEER_HALF], nt, preferred_element_type=F32)
    s2_ref[...] = lax.dot_general(k_ref[1], q_ref[:, PEER_HALF:2 * PEER_HALF], nt, preferred_element_type=F32)
    lane = 128
    n_tiles = s1_ref.shape[1] // lane

    def run(tiles, exact):
        outs = _select_tiles([s1_ref[:, c] for c in tiles], [s2_ref[:, c] for c in tiles], exact)
        for cols, (e1n, ni, e2, r2, _) in zip(tiles, outs):
            e1n_ref[0, :, cols] = e1n
            ni_ref[0, :, cols] = ni
            e2_ref[0, :, cols] = e2
            r2_ref[0, :, cols] = r2
        return [o[4] for o in outs]

    tiles = [slice(lt * lane, (lt + 1) * lane) for lt in range(n_tiles)]
    ties = run(tiles, exact=False)
    for cols, t in zip(tiles, ties):
        @pl.when(jnp.max(t) > 0.0)
        def _(cols=cols):
            run([cols], exact=True)


def _peersel(q2, keys_bf, tt):
    t = q2.shape[0]
    out_spec = pl.BlockSpec((1, N_KEYS, tt), lambda ti, h: (h, 0, ti))
    return pl.pallas_call(
        _peersel_body,
        grid=(t // tt, PEER_HEADS),
        in_specs=[pl.BlockSpec((tt, 2 * PEER_HALF), lambda ti, h: (ti, h)),
                  pl.BlockSpec((2, N_KEYS, PEER_HALF), lambda ti, h: (0, 0, 0))],
        out_specs=[out_spec] * 4,
        out_shape=[jax.ShapeDtypeStruct((PEER_HEADS, N_KEYS, t), F32)] * 2
        + [jax.ShapeDtypeStruct((PEER_HEADS, N_KEYS, t), BF16)] * 2,
        scratch_shapes=[pltpu.VMEM((N_KEYS, tt), F32), pltpu.VMEM((N_KEYS, tt), F32)],
        compiler_params=_params(("parallel", "parallel"), VMEM_LIMIT_MIB - VMEM_RESERVE_MIB),
        name="peersel",
    )(q2, keys_bf)


PEER_EB = 2048
PEER_EG = 512


def _gelu_sigmoid(x):
    k1 = 2.0 * math.sqrt(2.0 / math.pi) * math.log2(math.e)
    k2 = k1 * 0.044715
    z = x * (-k1 - k2 * (x * x))
    return x * (1.0 / (1.0 + jnp.exp2(z)))


def _peermix_body(h2_ref, u_ref, vt_ref, e1n_ref, ni_ref, e2_ref, r2_ref, x1_ref, mod_ref, lg_ref, lb_ref,
                  o_ref, acc_ref, w_ref, act_ref, g_ref, h2t_ref):
    e = pl.program_id(1)
    tt = acc_ref.shape[1]
    n_groups = PEER_EB // PEER_EG
    slabs_per_group = PEER_EG // N_KEYS
    pack = 16
    n_chunks = N_KEYS // pack
    lanes = 256
    zero = jnp.zeros((), BF16)

    @pl.when(e == 0)
    def _():
        acc_ref[...] = jnp.zeros(acc_ref.shape, F32)
        h2t_ref[...] = h2_ref[...].T

    def selection_weights(s):
        for lh in range(tt // lanes):
            cols = slice(lh * lanes, (lh + 1) * lanes)
            g = [None] * n_chunks
            for h in range(PEER_HEADS):
                nrow = jnp.broadcast_to(ni_ref[h, s:s + 1, cols], (pack, lanes)).astype(BF16)
                erow = jnp.broadcast_to(e1n_ref[h, s:s + 1, cols], (pack, lanes)).astype(BF16)
                for c in range(n_chunks):
                    rows = slice(c * pack, (c + 1) * pack)
                    term = jnp.where(r2_ref[h, rows, cols] < nrow, e2_ref[h, rows, cols], zero) * erow
                    g[c] = term if g[c] is None else g[c] + term
            for c in range(n_chunks):
                g_ref[s * N_KEYS + c * pack:s * N_KEYS + (c + 1) * pack, cols] = g[c]

    for p in range(n_groups):
        grp = slice(p * PEER_EG, (p + 1) * PEER_EG)
        act_ref[grp, :] = _dot(u_ref[grp, :], h2t_ref[...])
        for s in range(p * slabs_per_group, (p + 1) * slabs_per_group):
            selection_weights(s)
    for p in range(n_groups):
        grp = slice(p * PEER_EG, (p + 1) * PEER_EG)
        for c in range(PEER_EG // pack):
            rows = slice(p * PEER_EG + c * pack, p * PEER_EG + (c + 1) * pack)
            w_ref[rows, :] = _gelu_sigmoid(act_ref[rows, :]).astype(BF16) * g_ref[rows, :]
        acc_ref[...] += _dot(vt_ref[:, grp], w_ref[grp, :])

    @pl.when(e == pl.num_programs(1) - 1)
    def _():
        d = D_MODEL
        g2 = mod_ref[0][:, 5 * d:6 * d]
        y = acc_ref[...].T
        o_ref[...] = _layer_norm(DN_ALPHA * x1_ref[...] + g2 * y, lg_ref[...], lb_ref[...])


def _peermix(h2, u_bf, vt_bf, e1n, ni, e2, r2, x1, mod3, mod_row_of_tile, lg, lb, tt):
    t, d = h2.shape
    nb = u_bf.shape[0] // PEER_EB
    ns = PEER_EB // N_KEYS
    tile = lambda ti, e: (ti, 0)
    const = lambda ti, e: (0, 0)
    return pl.pallas_call(
        _peermix_body,
        grid=(t // tt, nb),
        in_specs=[pl.BlockSpec((tt, d), tile),
                  pl.BlockSpec((PEER_EB, d), lambda ti, e: (e, 0)),
                  pl.BlockSpec((d, PEER_EB), lambda ti, e: (0, e)),
                  pl.BlockSpec((PEER_HEADS, ns, tt), lambda ti, e: (0, e, ti)),
                  pl.BlockSpec((PEER_HEADS, ns, tt), lambda ti, e: (0, e, ti)),
                  pl.BlockSpec((PEER_HEADS, N_KEYS, tt), lambda ti, e: (0, 0, ti)),
                  pl.BlockSpec((PEER_HEADS, N_KEYS, tt), lambda ti, e: (0, 0, ti)),
                  pl.BlockSpec((tt, d), tile),
                  pl.BlockSpec((1, 1, 6 * d), lambda ti, e: (mod_row_of_tile(ti), 0, 0)),
                  pl.BlockSpec((1, d), const), pl.BlockSpec((1, d), const)],
        out_specs=pl.BlockSpec((tt, d), tile),
        out_shape=jax.ShapeDtypeStruct((t, d), F32),
        scratch_shapes=[pltpu.VMEM((d, tt), F32), pltpu.VMEM((PEER_EB, tt), BF16),
                        pltpu.VMEM((PEER_EB, tt), F32), pltpu.VMEM((PEER_EB, tt), BF16),
                        pltpu.VMEM((d, tt), BF16)],
        compiler_params=_params(("parallel", "arbitrary")),
        name="peermix",
    )(h2, u_bf, vt_bf, e1n, ni, e2, r2, x1, mod3, lg, lb)


def _prep_params(p):
    hd = RNN_HEAD_DIM
    q = dict(p)
    q["w_in_bf"] = p["w_in"].astype(BF16)
    q["wg"] = jnp.transpose(p["rnn_gate_w"], (2, 3, 0, 1, 4)).reshape(N_RNN_HEADS, hd, 4 * hd).astype(BF16)
    q["bg"] = jnp.transpose(p["rnn_gate_b"].reshape(2, 2, N_RNN_HEADS, hd), (2, 0, 1, 3)).reshape(
        N_RNN_HEADS, 1, 4 * hd)
    q["lam"] = jnp.transpose(p["rnn_lambda"].reshape(2, N_RNN_HEADS, hd), (1, 0, 2))
    q["rnn_cb"] = p["rnn_conv_b"][None, :]
    q["hy_cb"] = p["hy_conv_b"][None, :]
    q["wa"] = p["w_branch_a"].astype(BF16)
    q["wb"] = p["w_branch_b"].astype(BF16)
    q["wo"] = p["w_out"].astype(BF16)
    q["wq"] = p["peer_w_query"].astype(BF16)
    q["ln1_g2"] = p["ln1_g"][None, :]
    q["ln1_b2"] = p["ln1_b"][None, :]
    q["ln2_g2"] = p["ln2_g"][None, :]
    q["ln2_b2"] = p["ln2_b"][None, :]
    return q


def _mixer_group(x, pos, mod3, mod_row_of_batch, h0, q, tl, bb, dt, hy_lanes):
    b, l, _ = x.shape
    fwd, inv = _dft_matrices(l)
    kr, ki = _hyfilt(l, fwd, q)
    rx2, gg, hy, ga, gb = _inproj(x, pos, mod3, q["w_in_bf"], mod_row_of_batch, tl)
    y3, st = _rglru(rx2.reshape(l, b, D_RNN), h0, q["wg"], q["bg"], q["lam"], q["rnn_conv_w"],
                    q["rnn_cb"], bb)
    z = _hyena(hy, q["hy_conv_w"], q["hy_cb"], kr, ki, q["hy_skip"], fwd, inv, dt, hy_lanes)
    x1, h2, qq = _merge(x, pos, mod3, mod_row_of_batch, y3.reshape(l, b * D_RNN), gg, z, ga, gb,
                        q["wa"], q["wb"], q["wo"], q["ln1_g2"], q["ln1_b2"], q["wq"], tl)
    return x1, h2, qq, st


def _value_table_t(v):
    return v.T.astype(BF16)


def _peer_group(x1, h2, qq, mod3, mod_row_of_tile, q, tt_sel, tt_mix):
    b, l, d = x1.shape
    t = b * l
    e1n, ni, e2, r2 = _peersel(qq.reshape(t, qq.shape[-1]), q["keys_bf"], tt_sel)
    out = _peermix(h2.reshape(t, d), q["u_bf"], q["vt_bf"], e1n, ni, e2, r2, x1.reshape(t, d), mod3,
                   mod_row_of_tile, q["ln2_g2"], q["ln2_b2"], tt_mix)
    return out.reshape(b, l, d)


def _grid_pos_embed(n_tokens):
    rows = n_tokens // GRID_W
    t = np.arange(rows * GRID_W)
    r = (t // GRID_W).astype(np.float32)
    col = (t % GRID_W).astype(np.float32)
    quarter = D_MODEL // 4
    omega = (1.0 / (10000.0 ** (np.arange(quarter, dtype=np.float32) / np.float32(quarter)))).astype(np.float32)
    er = r[:, None] * omega[None, :]
    ec = col[:, None] * omega[None, :]
    return jnp.asarray(np.concatenate([np.sin(er), np.cos(er), np.sin(ec), np.cos(ec)], axis=-1), dtype=F32)


def _plan_tiles(batch, seq):
    tt = 2 * V7X_MXU_DIM
    tl = min(seq, tt)
    bb = min(batch, 4 * 8)
    dt = D_HY if seq * D_HY * 4 <= MIB else D_HY // 2
    hy_lanes = min(dt, 2 * V7X_MXU_DIM) if seq * D_HY * 4 <= MIB else V7X_MXU_DIM
    assert seq % tl == 0 and batch % bb == 0 and (batch * seq) % tt == 0 and (seq % tt == 0 or tt % seq == 0)
    return dict(tl=tl, bb=bb, dt=dt, hy_lanes=hy_lanes), tt


PARAM_NAMES = ("w_ada", "b_ada", "w_in", "rnn_conv_w", "rnn_conv_b", "rnn_gate_w", "rnn_gate_b", "rnn_lambda",
               "hy_conv_w", "hy_conv_b", "hy_ffn_w1", "hy_ffn_b1", "hy_ffn_w2", "hy_ffn_b2", "hy_ffn_w3",
               "hy_ffn_b3", "hy_sin_freq", "hy_skip", "w_branch_a", "w_branch_b", "w_out", "ln1_g", "ln1_b",
               "ln2_g", "ln2_b", "peer_w_query", "peer_sub_keys", "peer_u", "peer_v")


def kernel(x_prompt, x_sample, state_rglru, c, c_ctx, w_ada, b_ada, w_in, rnn_conv_w, rnn_conv_b, rnn_gate_w,
           rnn_gate_b, rnn_lambda, hy_conv_w, hy_conv_b, hy_ffn_w1, hy_ffn_b1, hy_ffn_w2, hy_ffn_b2, hy_ffn_w3,
           hy_ffn_b3, hy_sin_freq, hy_skip, w_branch_a, w_branch_b, w_out, ln1_g, ln1_b, ln2_g, ln2_b,
           peer_w_query, peer_sub_keys, peer_u, peer_v):
    stacked = dict(zip(PARAM_NAMES, (w_ada, b_ada, w_in, rnn_conv_w, rnn_conv_b, rnn_gate_w, rnn_gate_b,
                                     rnn_lambda, hy_conv_w, hy_conv_b, hy_ffn_w1, hy_ffn_b1, hy_ffn_w2,
                                     hy_ffn_b2, hy_ffn_w3, hy_ffn_b3, hy_sin_freq, hy_skip, w_branch_a,
                                     w_branch_b, w_out, ln1_g, ln1_b, ln2_g, ln2_b, peer_w_query,
                                     peer_sub_keys, peer_u, peer_v)))
    depth = w_ada.shape[0]
    bp, lp, d = x_prompt.shape
    bs, ls, _ = x_sample.shape
    assert bs + 1 <= MOD_ROWS
    cond = jnp.zeros((MOD_ROWS, d), F32).at[0].set(c_ctx).at[1:1 + bs].set(c)
    pos = _grid_pos_embed(ls)
    prompt_tiles, tt = _plan_tiles(bp, lp)
    sample_tiles, tt_sample = _plan_tiles(bs, ls)
    assert tt == tt_sample and ls % tt == 0
    sample_tiles_per_batch = ls // tt

    xp, xs = x_prompt, x_sample
    ctx_states = []
    for layer in range(depth):
        q = _prep_params({name: w[layer] for name, w in stacked.items()})
        q["keys_bf"] = q["peer_sub_keys"].astype(BF16)
        q["u_bf"] = q["peer_u"].astype(BF16)
        q["vt_bf"] = _value_table_t(q["peer_v"])
        mod3 = _ada(cond, q["w_ada"], q["b_ada"][None, :]).reshape(MOD_ROWS, 1, 6 * d)

        x1, h2, qq, st = _mixer_group(xp, None, mod3, lambda bi: 0, jnp.zeros((2, bp, D_RNN), F32), q,
                                      **prompt_tiles)
        xp = _peer_group(x1, h2, qq, mod3, lambda ti: 0, q, tt, tt)
        ctx_states.append(jnp.transpose(st, (1, 0, 2)))

        pos_l = pos if layer == 0 else None
        h0 = jnp.transpose(state_rglru[:, layer], (1, 0, 2))
        x1, h2, qq, _ = _mixer_group(xs, pos_l, mod3, lambda bi: bi + 1, h0, q, **sample_tiles)
        xs = _peer_group(x1, h2, qq, mod3, lambda ti: 1 + ti // sample_tiles_per_batch, q, tt, tt)

    new_state = jnp.stack(ctx_states, axis=1).astype(x_prompt.dtype)
    return (xp, xs, new_state)
```
